```python
import math
import jax, jax.numpy as jnp
from jax import lax
import numpy as np

D_MODEL = 1024
BATCH = 8
SEQ = 4096
DEPTH = 1

HEAD_DIM = 64
N_HEADS_DIL = 8
N_HEADS_SB = 8
D_DIL = N_HEADS_DIL * HEAD_DIM
D_SB = N_HEADS_SB * HEAD_DIM
D_MIX = D_DIL + D_SB
D_IN = 3 * D_DIL + 3 * D_SB
D_FF = -(-(8 * D_MODEL) // (3 * 256)) * 256
DILATED_PAIRS = ((128, 1), (512, 4), (2048, 16))
BLOCK = 128
ROPE_THETA = 10000.0
EPS = 1e-6

kernel_name = "hymba_dilated_stickbreaking_block"


def _rmsnorm(x, w):
    xf = x.astype(jnp.float32)
    y = xf * lax.rsqrt(jnp.mean(xf * xf, axis=-1, keepdims=True) + EPS)
    wb = w.astype(jnp.float32).reshape((1,) * (x.ndim - 1) + (w.shape[-1],))
    return (y * wb).astype(x.dtype)


def _rope_tables(seq_len):
    pos = jnp.arange(seq_len, dtype=jnp.float32)
    inv_freq = ROPE_THETA ** (-jnp.arange(0, HEAD_DIM, 2, dtype=jnp.float32) / HEAD_DIM)
    ang = pos[:, None] * inv_freq[None, :]
    return jnp.cos(ang)[None, None], jnp.sin(ang)[None, None]


def _apply_rope(x, cos, sin):
    xf = x.astype(jnp.float32)
    half = HEAD_DIM // 2
    x1, x2 = xf[..., :half], xf[..., half:]
    out = jnp.concatenate([x1 * cos - x2 * sin, x2 * cos + x1 * sin], axis=-1)
    return out.astype(x.dtype)


def _dilated_branch(q, k, v, window, dilation):
    B, H, S, Dh = q.shape
    r = dilation
    n_back = window // dilation
    L = S // r
    nb = -(-L // BLOCK)
    Lp = nb * BLOCK

    def strided(t):
        t = t.reshape(B, H, L, r, Dh).transpose(0, 1, 3, 2, 4)
        t = jnp.pad(t, ((0, 0), (0, 0), (0, 0), (0, Lp - L), (0, 0)))
        return t.reshape(B, H, r, nb, BLOCK, Dh)

    qb, kb, vb = strided(q), strided(k), strided(v)

    def with_prev(t):
        prev = jnp.pad(t, ((0, 0), (0, 0), (0, 0), (1, 0), (0, 0), (0, 0)))[:, :, :, :-1]
        return jnp.concatenate([prev, t], axis=-2)

    kk, vv = with_prev(kb), with_prev(vb)
    s = jnp.einsum('bhrnqd,bhrnkd->bhrnqk', qb, kk,
                   preferred_element_type=jnp.float32) * (Dh ** -0.5)
    i = jnp.arange(BLOCK)[:, None]
    j = jnp.arange(2 * BLOCK)[None, :]
    dist = i + BLOCK - j
    key_idx = (jnp.arange(nb)[:, None, None] - 1) * BLOCK + j[None]
    valid = (dist >= 0)[None] & (dist <= n_back)[None] & (key_idx >= 0)
    s = jnp.where(valid[None, None, None], s, -jnp.inf)
    m = jnp.max(s, axis=-1, keepdims=True)
    p = jnp.exp(s - m)
    den = jnp.sum(p, axis=-1, keepdims=True)
    o = jnp.einsum('bhrnqk,bhrnkd->bhrnqd', p.astype(v.dtype), vv,
                   preferred_element_type=jnp.float32) / den
    lse = (m + jnp.log(den))[..., 0]

    o = o.reshape(B, H, r, Lp, Dh)[:, :, :, :L].transpose(0, 1, 3, 2, 4).reshape(B, H, S, Dh)
    lse = lse.reshape(B, H, r, Lp)[:, :, :, :L].transpose(0, 1, 3, 2).reshape(B, H, S)
    return o, lse


def _dilated_attention(q, k, v):
    outs, lses = [], []
    for window, dilation in DILATED_PAIRS:
        o, lse = _dilated_branch(q, k, v, window, dilation)
        outs.append(o)
        lses.append(lse)
    w = jax.nn.softmax(jnp.stack(lses, axis=0), axis=0)
    o = jnp.sum(w[..., None] * jnp.stack(outs, axis=0), axis=0)
    return o.astype(q.dtype)


def _stick_breaking(q, k, v):
    B, H, S, Dh = q.shape
    nb = S // BLOCK
    scale = Dh ** -0.5
    qblocks = q.reshape(B, H, nb, BLOCK, Dh).transpose(2, 0, 1, 3, 4)
    kpos = jnp.arange(S)

    def one_block(args):
        qblk, bidx = args
        z = jnp.einsum('bhqd,bhkd->bhqk', qblk, k,
                       preferred_element_type=jnp.float32) * scale
        qpos = bidx * BLOCK + jnp.arange(BLOCK)
        causal = (kpos[None, :] < qpos[:, None])[None, None]
        log_beta = jax.nn.log_sigmoid(z)
        log_keep = jnp.where(causal, jax.nn.log_sigmoid(-z), 0.0)
        suffix = lax.cumsum(log_keep, axis=3, reverse=True) - log_keep
        a = jnp.where(causal, jnp.exp(log_beta + suffix), 0.0)
        return jnp.einsum('bhqk,bhkd->bhqd', a.astype(v.dtype), v,
                          preferred_element_type=jnp.float32).astype(v.dtype)

    out = lax.map(one_block, (qblocks, jnp.arange(nb)))
    return out.transpose(1, 2, 0, 3, 4).reshape(B, H, S, Dh)


def setup_inputs(seed: int = 0) -> dict:
    key = jax.random.key(seed)
    ks = jax.random.split(key, 12)
    f32 = jnp.float32

    def gain(k, n):
        return (1.0 + 0.02 * jax.random.normal(k, (DEPTH, n))).astype(f32)

    return {
        "x": jax.random.normal(ks[0], (BATCH, SEQ, D_MODEL), f32),
        "attn_norm_w": gain(ks[1], D_MODEL),
        "w_in": jax.random.normal(ks[2], (DEPTH, D_MODEL, D_IN), f32) * D_MODEL ** -0.5,
        "q_norm_w": gain(ks[3], HEAD_DIM),
        "k_norm_w": gain(ks[4], HEAD_DIM),
        "dil_out_norm_w": gain(ks[5], D_DIL),
        "sb_out_norm_w": gain(ks[6], D_SB),
        "w_out": jax.random.normal(ks[7], (DEPTH, D_MIX, D_MODEL), f32) * D_MIX ** -0.5,
        "ffn_norm_w": gain(ks[8], D_MODEL),
        "w_gate": jax.random.normal(ks[9], (DEPTH, D_MODEL, D_FF), f32) * D_MODEL ** -0.5,
        "w_up": jax.random.normal(ks[10], (DEPTH, D_MODEL, D_FF), f32) * D_MODEL ** -0.5,
        "w_down": jax.random.normal(ks[11], (DEPTH, D_FF, D_MODEL), f32) * D_FF ** -0.5,
    }


def reference(x, attn_norm_w, w_in, q_norm_w, k_norm_w, dil_out_norm_w, sb_out_norm_w,
              w_out, ffn_norm_w, w_gate, w_up, w_down):
    B, S, _ = x.shape
    cos, sin = _rope_tables(S)

    def heads(t, n):
        return t.reshape(B, S, n, HEAD_DIM).transpose(0, 2, 1, 3)

    def merge(t):
        return t.transpose(0, 2, 1, 3).reshape(B, S, t.shape[1] * t.shape[3])

    for l in range(DEPTH):
        h = _rmsnorm(x, attn_norm_w[l])
        proj = jnp.einsum('bsd,de->bse', h, w_in[l])
        qa = proj[..., 0:D_DIL]
        ka = proj[..., D_DIL:2 * D_DIL]
        va = proj[..., 2 * D_DIL:3 * D_DIL]
        o0 = 3 * D_DIL
        qs = proj[..., o0:o0 + D_SB]
        ksb = proj[..., o0 + D_SB:o0 + 2 * D_SB]
        vs = proj[..., o0 + 2 * D_SB:o0 + 3 * D_SB]

        qa = _apply_rope(_rmsnorm(heads(qa, N_HEADS_DIL), q_norm_w[l]), cos, sin)
        ka = _apply_rope(_rmsnorm(heads(ka, N_HEADS_DIL), k_norm_w[l]), cos, sin)
        o_dil = _dilated_attention(qa, ka, heads(va, N_HEADS_DIL))

        o_sb = _stick_breaking(heads(qs, N_HEADS_SB), heads(ksb, N_HEADS_SB),
                               heads(vs, N_HEADS_SB))

        mixed = jnp.concatenate([_rmsnorm(merge(o_dil), dil_out_norm_w[l]),
                                 _rmsnorm(merge(o_sb), sb_out_norm_w[l])], axis=-1)
        x = x + jnp.einsum('bse,ed->bsd', mixed, w_out[l])

        h = _rmsnorm(x, ffn_norm_w[l])
        g = jnp.einsum('bsd,df->bsf', h, w_gate[l])
        u = jnp.einsum('bsd,df->bsf', h, w_up[l])
        x = x + jnp.einsum('bsf,fd->bsd', jax.nn.silu(g) * u, w_down[l])
    return x
```

```python
import functools

import jax
import jax.numpy as jnp
from jax import lax
from jax.experimental import pallas as pl
from jax.experimental.pallas import tpu as pltpu

F32 = jnp.float32
BF16 = jnp.bfloat16

HEAD_DIM = 64
HALF = HEAD_DIM // 2
LANES = 128
MXU_DIM = 256
DILATIONS = (1, 4, 16)
N_BACK = 128
BLOCK = 128
SB_BLOCK = 256
ROPE_THETA = 10000.0
EPS = 1e-6
VMEM_LIMIT = 56 * 1024 * 1024


def _dot(a, b):
    return jnp.dot(a, b, preferred_element_type=F32)


def _dot_nt(a, b):
    return lax.dot_general(a, b, (((1,), (1,)), ((), ())), preferred_element_type=F32)


def _split_dot(x, m):
    hi = x.astype(BF16)
    lo = (x - hi.astype(F32)).astype(BF16)
    return _dot(hi, m) + _dot(lo, m)


def _rmsnorm(x, w):
    return x * lax.rsqrt(jnp.mean(x * x, axis=-1, keepdims=True) + EPS) * w


def _proj_kernel(x_ref, nw_ref, w_ref, g_ref, qw_ref, kw_ref, cos_ref, sin_ref,
                 qa_ref, ka_ref, va_ref, qs_ref, ks_ref, vs_ref):
    dg = qa_ref.shape[-1]
    h = _rmsnorm(x_ref[...], nw_ref[...]).astype(BF16)
    lane = lax.broadcasted_iota(jnp.int32, (1, LANES), 1)
    first_half = (lane % HEAD_DIM) < HALF
    cos = cos_ref[...]
    sin = sin_ref[...]

    def head_norm_rope(y, w_full):
        outs = []
        for j in range(dg // MXU_DIM):
            yc = y[:, j * MXU_DIM:(j + 1) * MXU_DIM]
            ss = _split_dot(yc * yc, g_ref[...])
            yn = yc * lax.rsqrt(ss * (1.0 / HEAD_DIM) + EPS) * w_full[:, j * MXU_DIM:(j + 1) * MXU_DIM]
            for i in range(MXU_DIM // LANES):
                v = yn[:, i * LANES:(i + 1) * LANES]
                partner = jnp.where(first_half, pltpu.roll(v, LANES - HALF, 1), pltpu.roll(v, HALF, 1))
                outs.append(v * cos + partner * sin)
        return jnp.concatenate(outs, axis=1)

    def proj(n):
        return _dot(h, w_ref[:, n * dg:(n + 1) * dg])

    scale = HEAD_DIM ** -0.5
    qa_ref[...] = (head_norm_rope(proj(0), qw_ref[...]) * scale).astype(BF16)
    ka_ref[...] = head_norm_rope(proj(1), kw_ref[...]).astype(BF16)
    va_ref[...] = proj(2).astype(BF16)
    qs_ref[...] = (proj(3) * scale).astype(BF16)
    ks_ref[...] = proj(4).astype(BF16)
    vs_ref[...] = proj(5).astype(BF16)


def _proj_call(x2, attn_norm_w, w_in, q_norm_w, k_norm_w, seq):
    t, d = x2.shape
    dg = w_in.shape[1] // 6
    tm = 512
    n_heads = dg // HEAD_DIM
    pos = jnp.arange(seq, dtype=F32)
    inv_freq = ROPE_THETA ** (-jnp.arange(0, HEAD_DIM, 2, dtype=F32) / HEAD_DIM)
    ang = pos[:, None] * inv_freq[None, :]
    cos_t = jnp.tile(jnp.cos(ang), (1, LANES // HALF))
    sin_t = jnp.tile(jnp.concatenate([-jnp.sin(ang), jnp.sin(ang)], axis=1), (1, LANES // HEAD_DIM))
    hid = jnp.arange(MXU_DIM) // HEAD_DIM
    g = (hid[:, None] == hid[None, :]).astype(BF16)
    qw = jnp.tile(q_norm_w.reshape(1, HEAD_DIM), (1, n_heads))
    kw = jnp.tile(k_norm_w.reshape(1, HEAD_DIM), (1, n_heads))
    tiles_per_seq = seq // tm
    row = lambda i: (i, 0)
    const = lambda i: (0, 0)
    out = jax.ShapeDtypeStruct((t, dg), BF16)
    return pl.pallas_call(
        _proj_kernel,
        grid=(t // tm,),
        in_specs=[
            pl.BlockSpec((tm, d), row),
            pl.BlockSpec((1, d), const),
            pl.BlockSpec(w_in.shape, const),
            pl.BlockSpec(g.shape, const),
            pl.BlockSpec((1, dg), const),
            pl.BlockSpec((1, dg), const),
            pl.BlockSpec((tm, LANES), lambda i: (i % tiles_per_seq, 0)),
            pl.BlockSpec((tm, LANES), lambda i: (i % tiles_per_seq, 0)),
        ],
        out_specs=[pl.BlockSpec((tm, dg), row)] * 6,
        out_shape=[out] * 6,
        compiler_params=pltpu.CompilerParams(
            dimension_semantics=("arbitrary",), vmem_limit_bytes=VMEM_LIMIT),
        name="proj",
    )(x2, attn_norm_w.reshape(1, d), w_in.astype(BF16), g, qw, kw, cos_t, sin_t)


def _dilated_kernel(*refs, has_prev, want_lse):
    q_ref, k_ref, v_ref = refs[:3]
    refs = refs[3:]
    if has_prev:
        op_ref, lp_ref = refs[:2]
        refs = refs[2:]
    o_ref = refs[0]
    l_ref = refs[1] if want_lse else None

    n_blocks = q_ref.shape[1] // BLOCK
    lane = lax.broadcasted_iota(jnp.int32, (1, LANES), 1)
    head0 = lane < HEAD_DIM
    qi = lax.broadcasted_iota(jnp.int32, (BLOCK, 2 * BLOCK), 0)
    kj = lax.broadcasted_iota(jnp.int32, (BLOCK, 2 * BLOCK), 1)
    dist = qi + BLOCK - kj
    band = (dist >= 0) & (dist <= N_BACK)
    tri = band[:, BLOCK:]

    def attend(q, kk, vv, valid):
        res = []
        for first in (True, False):
            qh = jnp.where(head0 == first, q, jnp.zeros_like(q))
            s = jnp.where(valid, _dot_nt(qh, kk), -jnp.inf)
            m = jnp.max(s, axis=-1, keepdims=True)
            p = jnp.exp(s - m)
            den = jnp.sum(p, axis=-1, keepdims=True)
            res.append((_dot(p.astype(BF16), vv) / den, m + jnp.log(den)))
        o = jnp.where(head0, res[0][0], res[1][0])
        lse = jnp.where(head0, res[0][1], res[1][1])
        return o, lse

    def finish(rows, o, lse):
        if has_prev:
            o_p = op_ref[0, rows, :]
            l_p = lp_ref[0, rows, :]
            mx = jnp.maximum(l_p, lse)
            w_p = jnp.exp(l_p - mx)
            w_n = jnp.exp(lse - mx)
            tot = w_p + w_n
            o = (o_p * w_p + o * w_n) / tot
            lse = mx + jnp.log(tot)
        o_ref[0, rows, :] = o
        if want_lse:
            l_ref[0, rows, :] = lse

    rows0 = pl.ds(0, BLOCK)
    o, lse = attend(q_ref[0, rows0, :], k_ref[0, rows0, :], v_ref[0, rows0, :], tri)
    finish(rows0, o, lse)

    def body(n, carry):
        rows = pl.ds(pl.multiple_of(n * BLOCK, BLOCK), BLOCK)
        keys = pl.ds(pl.multiple_of((n - 1) * BLOCK, BLOCK), 2 * BLOCK)
        o, lse = attend(q_ref[0, rows, :], k_ref[0, keys, :], v_ref[0, keys, :], band)
        finish(rows, o, lse)
        return carry

    lax.fori_loop(1, n_blocks, body, 0)


def _dilated_call(r, qa, ka, va, prev, want_lse):
    b, s, dg = qa.shape
    l = s // r
    pairs = dg // LANES
    view = lambda a: a.reshape(b, l, r * dg)
    spec = pl.BlockSpec((1, l, LANES), lambda bi, c, hp: (bi, 0, c * pairs + hp))
    ins = [view(qa), view(ka), view(va)]
    if prev is not None:
        ins += [view(prev[0]), view(prev[1])]
    f32_out = jax.ShapeDtypeStruct((b, l, r * dg), F32)
    outs = pl.pallas_call(
        functools.partial(_dilated_kernel, has_prev=prev is not None, want_lse=want_lse),
        grid=(b, r, pairs),
        in_specs=[spec] * len(ins),
        out_specs=[spec] * (2 if want_lse else 1),
        out_shape=[f32_out] * (2 if want_lse else 1),
        compiler_params=pltpu.CompilerParams(
            dimension_semantics=("arbitrary",) * 3, vmem_limit_bytes=VMEM_LIMIT),
        name=f"dilated_r{r}",
    )(*ins)
    return [o.reshape(b, s, dg) for o in outs]


def _sb_kernel(q_ref, k_ref, v_ref, u_ref, o_ref):
    blk = q_ref.shape[1]
    qb = pl.program_id(2)
    lane = lax.broadcasted_iota(jnp.int32, (1, LANES), 1)
    head0 = lane < HEAD_DIM
    ri = lax.broadcasted_iota(jnp.int32, (blk, blk), 0)
    cj = lax.broadcasted_iota(jnp.int32, (blk, blk), 1)
    causal = cj < ri
    q = q_ref[0]
    u = u_ref[...]

    def log_gates(qh, kb):
        keys = pl.ds(pl.multiple_of(kb * blk, blk), blk)
        z = _dot_nt(qh, k_ref[0, keys, :])
        log_beta = jnp.minimum(z, 0.0) - jnp.log(1.0 + jnp.exp(-jnp.abs(z)))
        return keys, log_beta, log_beta - z

    accs = []
    for first in (True, False):
        qh = jnp.where(head0 == first, q, jnp.zeros_like(q))

        keys, log_beta, log_keep = log_gates(qh, qb)
        log_keep = jnp.where(causal, log_keep, 0.0)
        a = jnp.where(causal, jnp.exp(log_beta + _split_dot(log_keep, u)), 0.0)
        acc = _dot(a.astype(BF16), v_ref[0, keys, :])
        carry = jnp.sum(log_keep, axis=-1, keepdims=True)

        def body(i, state):
            carry, acc = state
            keys, log_beta, log_keep = log_gates(qh, qb - 1 - i)
            a = jnp.exp(log_beta + _split_dot(log_keep, u) + carry)
            acc = acc + _dot(a.astype(BF16), v_ref[0, keys, :])
            return carry + jnp.sum(log_keep, axis=-1, keepdims=True), acc

        carry, acc = lax.fori_loop(0, qb, body, (carry, acc))
        accs.append(acc)
    o_ref[0] = jnp.where(head0, accs[0], accs[1])


def _sb_call(qs, ks, vs):
    b, s, dg = qs.shape
    blk = SB_BLOCK
    pairs = dg // LANES
    idx = jnp.arange(blk)
    u = (idx[:, None] > idx[None, :]).astype(BF16)
    kv_spec = pl.BlockSpec((1, s, LANES), lambda bi, hp, qb: (bi, 0, hp))
    q_spec = pl.BlockSpec((1, blk, LANES), lambda bi, hp, qb: (bi, qb, hp))
    return pl.pallas_call(
        _sb_kernel,
        grid=(b, pairs, s // blk),
        in_specs=[q_spec, kv_spec, kv_spec, pl.BlockSpec((blk, blk), lambda bi, hp, qb: (0, 0))],
        out_specs=q_spec,
        out_shape=jax.ShapeDtypeStruct((b, s, dg), F32),
        compiler_params=pltpu.CompilerParams(
            dimension_semantics=("arbitrary",) * 3, vmem_limit_bytes=VMEM_LIMIT),
        name="stick_breaking",
    )(qs, ks, vs, u)


def _out_ffn_kernel(x_ref, od_ref, os_ref, dw_ref, sw_ref, wo_ref, fw_ref, wg_ref, wu_ref, wd_ref,
                    out_ref, acc_ref, *, f_chunk):
    dg = od_ref.shape[-1]
    nd = _rmsnorm(od_ref[...], dw_ref[...]).astype(BF16)
    ns = _rmsnorm(os_ref[...], sw_ref[...]).astype(BF16)
    x1 = x_ref[...] + _dot(nd, wo_ref[:dg, :]) + _dot(ns, wo_ref[dg:, :])
    h = _rmsnorm(x1, fw_ref[...]).astype(BF16)
    acc_ref[...] = x1

    def body(i, carry):
        cols = pl.ds(pl.multiple_of(i * f_chunk, f_chunk), f_chunk)
        g = _dot(h, wg_ref[:, cols])
        up = _dot(h, wu_ref[:, cols])
        act = (g / (1.0 + jnp.exp(-g)) * up).astype(BF16)
        acc_ref[...] += _dot(act, wd_ref[cols, :])
        return carry

    lax.fori_loop(0, wg_ref.shape[1] // f_chunk, body, 0)
    out_ref[...] = acc_ref[...]


def _out_ffn_call(x2, o_dil, o_sb, dil_w, sb_w, w_out, ffn_w, w_gate, w_up, w_down):
    t, d = x2.shape
    dg = o_dil.shape[1]
    d_ff = w_gate.shape[1]
    tm = 512
    f_chunk = 256
    row = lambda i: (i, 0)
    const = lambda i: (0, 0)
    resident = lambda shape: pl.BlockSpec(shape, const, pipeline_mode=pl.Buffered(1))
    return pl.pallas_call(
        functools.partial(_out_ffn_kernel, f_chunk=f_chunk),
        grid=(t // tm,),
        in_specs=[
            pl.BlockSpec((tm, d), row),
            pl.BlockSpec((tm, dg), row),
            pl.BlockSpec((tm, dg), row),
            pl.BlockSpec((1, dg), const),
            pl.BlockSpec((1, dg), const),
            resident((2 * dg, d)),
            pl.BlockSpec((1, d), const),
            resident((d, d_ff)),
            resident((d, d_ff)),
            resident((d_ff, d)),
        ],
        out_specs=pl.BlockSpec((tm, d), row),
        out_shape=jax.ShapeDtypeStruct((t, d), F32),
        scratch_shapes=[pltpu.VMEM((tm, d), F32)],
        compiler_params=pltpu.CompilerParams(
            dimension_semantics=("arbitrary",), vmem_limit_bytes=VMEM_LIMIT),
        name="out_ffn",
    )(x2, o_dil, o_sb, dil_w.reshape(1, dg), sb_w.reshape(1, dg), w_out.astype(BF16),
      ffn_w.reshape(1, d), w_gate.astype(BF16), w_up.astype(BF16), w_down.astype(BF16))


def kernel(x, attn_norm_w, w_in, q_norm_w, k_norm_w, dil_out_norm_w, sb_out_norm_w, w_out,
           ffn_norm_w, w_gate, w_up, w_down):
    b, s, d = x.shape
    for l in range(attn_norm_w.shape[0]):
        x2 = x.reshape(b * s, d)
        qa, ka, va, qs, ks, vs = _proj_call(x2, attn_norm_w[l], w_in[l], q_norm_w[l], k_norm_w[l], s)
        dg = qa.shape[1]
        qa, ka, va, qs, ks, vs = (a.reshape(b, s, dg) for a in (qa, ka, va, qs, ks, vs))

        state = None
        for n, r in enumerate(DILATIONS):
            last = n == len(DILATIONS) - 1
            state = _dilated_call(r, qa, ka, va, state, want_lse=not last)
        o_dil = state[0]
        o_sb = _sb_call(qs, ks, vs)

        x = _out_ffn_call(x2, o_dil.reshape(b * s, dg), o_sb.reshape(b * s, dg), dil_out_norm_w[l],
                          sb_out_norm_w[l], w_out[l], ffn_norm_w[l], w_gate[l], w_up[l],
                          w_down[l]).reshape(b, s, d)
    return x
```

```python
import functools

import jax
import jax.numpy as jnp
from jax import lax
from jax.experimental import pallas as pl
from jax.experimental.pallas import tpu as pltpu

F32 = jnp.float32
BF16 = jnp.bfloat16

HEAD_DIM = 64
HALF = HEAD_DIM // 2
LANES = 128
MXU_DIM = 256
DILATIONS = (1, 4, 16)
N_BACK = 128
BLOCK = 128
DIL_GROUP = 4
SB_Q = 128
SB_K = MXU_DIM
SB_GROUP = 4
SB_PAD = (SB_GROUP - 1) * SB_Q
SB_DEAD = 88.0
ROPE_THETA = 10000.0
EPS = 1e-6
VMEM_LIMIT = 56 * 1024 * 1024


def _dot(a, b):
    return jnp.dot(a, b, preferred_element_type=F32)


def _dot_nt(a, b):
    return lax.dot_general(a, b, (((1,), (1,)), ((), ())), preferred_element_type=F32)


def _split_dot(x, m):
    hi = x.astype(BF16)
    lo = (x - hi.astype(F32)).astype(BF16)
    return _dot(hi, m) + _dot(lo, m)


def _rmsnorm(x, w):
    return x * lax.rsqrt(jnp.mean(x * x, axis=-1, keepdims=True) + EPS) * w


def _proj_kernel(x_ref, nw_ref, w_ref, g_ref, qw_ref, kw_ref, cos_ref, sin_ref,
                 qa_ref, ka_ref, va_ref, qs_ref, ks_ref, vs_ref):
    dg = qa_ref.shape[-1]
    h = _rmsnorm(x_ref[...], nw_ref[...]).astype(BF16)
    lane = lax.broadcasted_iota(jnp.int32, (1, LANES), 1)
    first_half = (lane % HEAD_DIM) < HALF
    cos = cos_ref[...]
    sin = sin_ref[...]

    def head_norm_rope(y, w_full):
        outs = []
        for j in range(dg // MXU_DIM):
            yc = y[:, j * MXU_DIM:(j + 1) * MXU_DIM]
            ss = _split_dot(yc * yc, g_ref[...])
            yn = yc * lax.rsqrt(ss * (1.0 / HEAD_DIM) + EPS) * w_full[:, j * MXU_DIM:(j + 1) * MXU_DIM]
            for i in range(MXU_DIM // LANES):
                v = yn[:, i * LANES:(i + 1) * LANES]
                partner = jnp.where(first_half, pltpu.roll(v, LANES - HALF, 1), pltpu.roll(v, HALF, 1))
                outs.append(v * cos + partner * sin)
        return jnp.concatenate(outs, axis=1)

    def proj(n):
        return _dot(h, w_ref[:, n * dg:(n + 1) * dg])

    scale = HEAD_DIM ** -0.5
    qa_ref[...] = (head_norm_rope(proj(0), qw_ref[...]) * scale).astype(BF16)
    ka_ref[...] = head_norm_rope(proj(1), kw_ref[...]).astype(BF16)
    va_ref[...] = proj(2).astype(BF16)
    qs_ref[...] = (proj(3) * scale).astype(BF16)
    ks_ref[...] = proj(4).astype(BF16)
    vs_ref[...] = proj(5).astype(BF16)


def _proj_call(x2, attn_norm_w, w_in, q_norm_w, k_norm_w, seq):
    t, d = x2.shape
    dg = w_in.shape[1] // 6
    tm = 512
    n_heads = dg // HEAD_DIM
    pos = jnp.arange(seq, dtype=F32)
    inv_freq = ROPE_THETA ** (-jnp.arange(0, HEAD_DIM, 2, dtype=F32) / HEAD_DIM)
    ang = pos[:, None] * inv_freq[None, :]
    cos_t = jnp.tile(jnp.cos(ang), (1, LANES // HALF))
    sin_t = jnp.tile(jnp.concatenate([-jnp.sin(ang), jnp.sin(ang)], axis=1), (1, LANES // HEAD_DIM))
    hid = jnp.arange(MXU_DIM) // HEAD_DIM
    g = (hid[:, None] == hid[None, :]).astype(BF16)
    qw = jnp.tile(q_norm_w.reshape(1, HEAD_DIM), (1, n_heads))
    kw = jnp.tile(k_norm_w.reshape(1, HEAD_DIM), (1, n_heads))
    tiles_per_seq = seq // tm
    row = lambda i: (i, 0)
    const = lambda i: (0, 0)
    out = jax.ShapeDtypeStruct((t, dg), BF16)
    return pl.pallas_call(
        _proj_kernel,
        grid=(t // tm,),
        in_specs=[
            pl.BlockSpec((tm, d), row),
            pl.BlockSpec((1, d), const),
            pl.BlockSpec(w_in.shape, const),
            pl.BlockSpec(g.shape, const),
            pl.BlockSpec((1, dg), const),
            pl.BlockSpec((1, dg), const),
            pl.BlockSpec((tm, LANES), lambda i: (i % tiles_per_seq, 0)),
            pl.BlockSpec((tm, LANES), lambda i: (i % tiles_per_seq, 0)),
        ],
        out_specs=[pl.BlockSpec((tm, dg), row)] * 6,
        out_shape=[out] * 6,
        compiler_params=pltpu.CompilerParams(
            dimension_semantics=("arbitrary",), vmem_limit_bytes=VMEM_LIMIT),
        name="proj",
    )(x2, attn_norm_w.reshape(1, d), w_in.astype(BF16), g, qw, kw, cos_t, sin_t)


def _dilated_kernel(q_ref, k_ref, v_ref, o_ref, f_ref, f4_ref, q4_ref, k4_ref, v4_ref,
                    q16_ref, k16_ref, v16_ref, oa_ref, la_ref, ob_ref, lb_ref):
    s = q_ref.shape[1]
    quarter, sixteenth = s // 4, s // 16

    for src, d4, d16 in ((q_ref, q4_ref, q16_ref), (k_ref, k4_ref, k16_ref), (v_ref, v4_ref, v16_ref)):
        f_ref[...] = src[0].astype(F32)
        for c in range(4):
            chunk = f_ref[pl.ds(c, quarter, stride=4), :]
            f4_ref[pl.ds(c * quarter, quarter), :] = chunk
            d4[pl.ds(c * quarter, quarter), :] = chunk.astype(BF16)
        for c in range(16):
            chunk = f4_ref[pl.ds((c % 4) * quarter + c // 4, sixteenth, stride=4), :]
            d16[pl.ds(c * sixteenth, sixteenth), :] = chunk.astype(BF16)

    lane = lax.broadcasted_iota(jnp.int32, (1, LANES), 1)
    head0 = lane < HEAD_DIM
    qi = lax.broadcasted_iota(jnp.int32, (2 * BLOCK, 2 * BLOCK), 0) % BLOCK
    kj = lax.broadcasted_iota(jnp.int32, (2 * BLOCK, 2 * BLOCK), 1)
    dist = qi + BLOCK - kj
    band = (dist >= 0) & (dist <= N_BACK)
    tri = band[:, BLOCK:]

    def attend(load_q, k_src, v_src, blocks):
        qs, keys, masks = [], [], []
        for start, first in blocks:
            q = load_q(pl.ds(start, BLOCK))
            zero = jnp.zeros_like(q)
            qs.append(jnp.concatenate([jnp.where(head0, q, zero), jnp.where(head0, zero, q)], axis=0))
            keys.append(pl.ds(start, BLOCK) if first else pl.ds(_aligned(start - BLOCK), 2 * BLOCK))
            masks.append(tri if first else band)
        ss = [jnp.where(m, _dot_nt(q, k_src[k, :]), -jnp.inf) for q, k, m in zip(qs, keys, masks)]
        ms = [jnp.max(sc, axis=-1, keepdims=True) for sc in ss]
        ps = [jnp.exp(sc - m) for sc, m in zip(ss, ms)]
        dens = [jnp.sum(p, axis=-1, keepdims=True) for p in ps]
        pvs = [_dot(p.astype(BF16), v_src[k, :]) for p, k in zip(ps, keys)]
        outs = []
        for pv, m, den in zip(pvs, ms, dens):
            o = pv / den
            lse = m + jnp.log(den)
            outs.append((jnp.where(head0, o[:BLOCK], o[BLOCK:]), jnp.where(head0, lse[:BLOCK], lse[BLOCK:])))
        return outs

    def fold(o_p, l_p, o, lse):
        mx = jnp.maximum(l_p, lse)
        w_p = jnp.exp(l_p - mx)
        w_n = jnp.exp(lse - mx)
        tot = w_p + w_n
        return (o_p * w_p + o * w_n) / tot, mx + jnp.log(tot)

    def sweep(per_seq, group):
        total = s // BLOCK
        if per_seq <= 2 * DIL_GROUP:
            n = per_seq * max(1, DIL_GROUP // per_seq)
            flags = ((True,) + (False,) * (per_seq - 1)) * (n // per_seq)
            lo = 0
        else:
            n = DIL_GROUP
            flags = (False,) * n
            group(0, (True,) + flags[1:])
            lo = 1

        def body(t, carry):
            group(t * (n * BLOCK), flags)
            return carry
        lax.fori_loop(lo, total // n, body, 0)

    def rows_of(base, n):
        return [_aligned(base + j * BLOCK) for j in range(n)]

    def group16(base, flags):
        st = rows_of(base, len(flags))
        res = attend(lambda r: q16_ref[r, :], k16_ref, v16_ref, list(zip(st, flags)))
        for start, (o, lse) in zip(st, res):
            oa_ref[pl.ds(start, BLOCK), :] = o
            la_ref[pl.ds(start, BLOCK), :] = lse
    sweep(sixteenth // BLOCK, group16)
    for c in range(16):
        dst = pl.ds((c % 4) * quarter + c // 4, sixteenth, stride=4)
        ob_ref[dst, :] = oa_ref[pl.ds(c * sixteenth, sixteenth), :]
        lb_ref[dst, :] = la_ref[pl.ds(c * sixteenth, sixteenth), :]

    def group4(base, flags):
        st = rows_of(base, len(flags))
        res = attend(lambda r: q4_ref[r, :], k4_ref, v4_ref, list(zip(st, flags)))
        for start, (o, lse) in zip(st, res):
            rows = pl.ds(start, BLOCK)
            o, lse = fold(ob_ref[rows, :], lb_ref[rows, :], o, lse)
            ob_ref[rows, :] = o
            lb_ref[rows, :] = lse
    sweep(quarter // BLOCK, group4)
    for c in range(4):
        oa_ref[pl.ds(c, quarter, stride=4), :] = ob_ref[pl.ds(c * quarter, quarter), :]
        la_ref[pl.ds(c, quarter, stride=4), :] = lb_ref[pl.ds(c * quarter, quarter), :]

    def group1(base, flags):
        st = rows_of(base, len(flags))
        res = attend(lambda r: q_ref[0, r, :], k_ref.at[0], v_ref.at[0], list(zip(st, flags)))
        for start, (o, lse) in zip(st, res):
            rows = pl.ds(start, BLOCK)
            o_ref[0, rows, :] = fold(oa_ref[rows, :], la_ref[rows, :], o, lse)[0]
    sweep(s // BLOCK, group1)


def _aligned(i):
    return i if isinstance(i, int) else pl.multiple_of(i, BLOCK)


def _dilated_call(qa, ka, va):
    b, s, dg = qa.shape
    pairs = dg // LANES
    spec = pl.BlockSpec((1, s, LANES), lambda bi, hp: (bi, 0, hp))
    slab = lambda dt: pltpu.VMEM((s, LANES), dt)
    return pl.pallas_call(
        _dilated_kernel,
        grid=(b, pairs),
        in_specs=[spec] * 3,
        out_specs=spec,
        out_shape=jax.ShapeDtypeStruct((b, s, dg), F32),
        scratch_shapes=[slab(F32)] * 2 + [slab(BF16)] * 6 + [slab(F32)] * 4,
        compiler_params=pltpu.CompilerParams(
            dimension_semantics=("arbitrary",) * 2, vmem_limit_bytes=VMEM_LIMIT),
        name="dilated",
    )(qa, ka, va)


def _sb_kernel(q_ref, k_ref, v_ref, u_ref, o_ref, kp_ref, vp_ref):
    s = q_ref.shape[1]
    kp_ref[pl.ds(0, SB_PAD), :] = jnp.zeros((SB_PAD, LANES), BF16)
    vp_ref[pl.ds(0, SB_PAD), :] = jnp.zeros((SB_PAD, LANES), BF16)
    kp_ref[pl.ds(SB_PAD, s), :] = k_ref[0]
    vp_ref[pl.ds(SB_PAD, s), :] = v_ref[0]

    lane = lax.broadcasted_iota(jnp.int32, (1, LANES), 1)
    head0 = lane < HEAD_DIM
    ri = lax.broadcasted_iota(jnp.int32, (2 * SB_Q, SB_K), 0) % SB_Q
    cj = lax.broadcasted_iota(jnp.int32, (2 * SB_Q, SB_K), 1)
    causal = cj < ri + (SB_K - SB_Q)
    u = u_ref[...]

    def gates(z, mask):
        neg_keep = jnp.maximum(z, 0.0) + jnp.log(1.0 + jnp.exp(-jnp.abs(z)))
        log_beta = z - neg_keep
        if mask is not None:
            neg_keep = jnp.where(mask, neg_keep, 0.0)
        return neg_keep, log_beta

    def suffix(neg_keep):
        hi = neg_keep.astype(BF16)
        lo = (neg_keep - hi.astype(F32)).astype(BF16)
        cs = _dot(jnp.concatenate([hi, lo], axis=0), u)
        return cs[:2 * SB_Q] + cs[2 * SB_Q:]

    def weights(log_beta, suf, shift, mask):
        x = log_beta - suf
        if shift is not None:
            x = x - shift
        a = jnp.exp(x)
        if mask is not None:
            a = jnp.where(mask, a, 0.0)
        return a.astype(BF16)

    def sweep(qs, starts, shifts, mask):
        keys = [pl.ds(pl.multiple_of(st, SB_Q), SB_K) for st in starts]
        zs = [_dot_nt(q, kp_ref[k, :]) for q, k in zip(qs, keys)]
        gs = [gates(z, mask) for z in zs]
        sufs = [suffix(nk) for nk, _ in gs]
        ws = [weights(lb, suf, sh, mask) for (_, lb), suf, sh in zip(gs, sufs, shifts)]
        pvs = [_dot(w, vp_ref[k, :]) for w, k in zip(ws, keys)]
        tots = [jnp.sum(nk, axis=-1, keepdims=True) for nk, _ in gs]
        return pvs, tots

    def live(shifts):
        return jnp.min(functools.reduce(jnp.minimum, shifts)) < SB_DEAD

    def group_body(p, carry):
        qbs = [SB_GROUP * p + j for j in range(SB_GROUP)]
        qs = []
        for qb in qbs:
            q = q_ref[0, pl.ds(pl.multiple_of(qb * SB_Q, SB_Q), SB_Q), :]
            zero = jnp.zeros_like(q)
            qs.append(jnp.concatenate([jnp.where(head0, q, zero), jnp.where(head0, zero, q)], axis=0))
        pad_blocks = SB_PAD // SB_Q
        accs, shifts = sweep(qs, [(qb - 1 + pad_blocks) * SB_Q for qb in qbs], [None] * SB_GROUP, causal)

        def cond(state):
            return (state[0] <= (SB_GROUP * p + SB_GROUP - 1) // 2) & state[1]

        def body(state):
            i, _, accs, shifts = state
            pvs, tots = sweep(qs, [(qb - 1 + pad_blocks - 2 * i) * SB_Q for qb in qbs], shifts, None)
            accs = tuple(a + pv for a, pv in zip(accs, pvs))
            shifts = tuple(sh + t for sh, t in zip(shifts, tots))
            return i + 1, live(shifts), accs, shifts

        state = (jnp.int32(1), live(shifts), tuple(accs), tuple(shifts))
        accs = lax.while_loop(cond, body, state)[2]
        for qb, acc in zip(qbs, accs):
            rows = pl.ds(pl.multiple_of(qb * SB_Q, SB_Q), SB_Q)
            o_ref[0, rows, :] = jnp.where(head0, acc[:SB_Q], acc[SB_Q:])
        return carry

    lax.fori_loop(0, s // (SB_GROUP * SB_Q), group_body, 0)


def _sb_call(qs, ks, vs):
    b, s, dg = qs.shape
    pairs = dg // LANES
    idx = jnp.arange(SB_K)
    u = (idx[:, None] > idx[None, :]).astype(BF16)
    spec = pl.BlockSpec((1, s, LANES), lambda bi, hp: (bi, 0, hp))
    return pl.pallas_call(
        _sb_kernel,
        grid=(b, pairs),
        in_specs=[spec, spec, spec, pl.BlockSpec((SB_K, SB_K), lambda bi, hp: (0, 0))],
        out_specs=spec,
        out_shape=jax.ShapeDtypeStruct((b, s, dg), F32),
        scratch_shapes=[pltpu.VMEM((s + SB_PAD, LANES), BF16)] * 2,
        compiler_params=pltpu.CompilerParams(
            dimension_semantics=("arbitrary",) * 2, vmem_limit_bytes=VMEM_LIMIT),
        name="stick_breaking",
    )(qs, ks, vs, u)


def _out_ffn_kernel(x_ref, od_ref, os_ref, dw_ref, sw_ref, wo_ref, fw_ref, wg_ref, wu_ref, wd_ref,
                    out_ref, acc_ref, *, f_chunk):
    dg = od_ref.shape[-1]
    nd = _rmsnorm(od_ref[...], dw_ref[...]).astype(BF16)
    ns = _rmsnorm(os_ref[...], sw_ref[...]).astype(BF16)
    x1 = x_ref[...] + _dot(nd, wo_ref[:dg, :]) + _dot(ns, wo_ref[dg:, :])
    h = _rmsnorm(x1, fw_ref[...]).astype(BF16)
    acc_ref[...] = x1

    def body(i, carry):
        cols = pl.ds(pl.multiple_of(i * f_chunk, f_chunk), f_chunk)
        g = _dot(h, wg_ref[:, cols])
        up = _dot(h, wu_ref[:, cols])
        act = (g / (1.0 + jnp.exp(-g)) * up).astype(BF16)
        acc_ref[...] += _dot(act, wd_ref[cols, :])
        return carry

    lax.fori_loop(0, wg_ref.shape[1] // f_chunk, body, 0)
    out_ref[...] = acc_ref[...]


def _out_ffn_call(x2, o_dil, o_sb, dil_w, sb_w, w_out, ffn_w, w_gate, w_up, w_down):
    t, d = x2.shape
    dg = o_dil.shape[1]
    d_ff = w_gate.shape[1]
    tm = 512
    f_chunk = 256
    row = lambda i: (i, 0)
    const = lambda i: (0, 0)
    resident = lambda shape: pl.BlockSpec(shape, const, pipeline_mode=pl.Buffered(1))
    return pl.pallas_call(
        functools.partial(_out_ffn_kernel, f_chunk=f_chunk),
        grid=(t // tm,),
        in_specs=[
            pl.BlockSpec((tm, d), row),
            pl.BlockSpec((tm, dg), row),
            pl.BlockSpec((tm, dg), row),
            pl.BlockSpec((1, dg), const),
            pl.BlockSpec((1, dg), const),
            resident((2 * dg, d)),
            pl.BlockSpec((1, d), const),
            resident((d, d_ff)),
            resident((d, d_ff)),
            resident((d_ff, d)),
        ],
        out_specs=pl.BlockSpec((tm, d), row),
        out_shape=jax.ShapeDtypeStruct((t, d), F32),
        scratch_shapes=[pltpu.VMEM((tm, d), F32)],
        compiler_params=pltpu.CompilerParams(
            dimension_semantics=("arbitrary",), vmem_limit_bytes=VMEM_LIMIT),
        name="out_ffn",
    )(x2, o_dil, o_sb, dil_w.reshape(1, dg), sb_w.reshape(1, dg), w_out.astype(BF16),
      ffn_w.reshape(1, d), w_gate.astype(BF16), w_up.astype(BF16), w_down.astype(BF16))


def kernel(x, attn_norm_w, w_in, q_norm_w, k_norm_w, dil_out_norm_w, sb_out_norm_w, w_out,
           ffn_norm_w, w_gate, w_up, w_down):
    b, s, d = x.shape
    for l in range(attn_norm_w.shape[0]):
        x2 = x.reshape(b * s, d)
        qa, ka, va, qs, ks, vs = _proj_call(x2, attn_norm_w[l], w_in[l], q_norm_w[l], k_norm_w[l], s)
        dg = qa.shape[1]
        qa, ka, va, qs, ks, vs = (a.reshape(b, s, dg) for a in (qa, ka, va, qs, ks, vs))

        o_dil = _dilated_call(qa, ka, va)
        o_sb = _sb_call(qs, ks, vs)

        x = _out_ffn_call(x2, o_dil.reshape(b * s, dg), o_sb.reshape(b * s, dg), dil_out_norm_w[l],
                          sb_out_norm_w[l], w_out[l], ffn_norm_w[l], w_gate[l], w_up[l],
                          w_down[l]).reshape(b, s, d)
    return x
```

```python
import functools

import jax
import jax.numpy as jnp
from jax import lax
from jax.experimental import pallas as pl
from jax.experimental.pallas import tpu as pltpu

F32 = jnp.float32
BF16 = jnp.bfloat16

HEAD_DIM = 64
HALF = HEAD_DIM // 2
LANES = 128
MXU_DIM = 256
DILATIONS = (1, 4, 16)
N_BACK = 128
BLOCK = 128
DIL_GROUP = 4
SB_Q = 128
SB_K = MXU_DIM
SB_GROUP = 4
SB_PAD = (SB_GROUP - 1) * SB_Q
SB_DEAD = 127.0
LOG2E = 1.4426950408889634
ROPE_THETA = 10000.0
EPS = 1e-6
VMEM_LIMIT = 56 * 1024 * 1024


def _dot(a, b):
    return jnp.dot(a, b, preferred_element_type=F32)


def _dot_nt(a, b):
    return lax.dot_general(a, b, (((1,), (1,)), ((), ())), preferred_element_type=F32)


def _split_dot(x, m):
    hi = x.astype(BF16)
    lo = (x - hi.astype(F32)).astype(BF16)
    return _dot(hi, m) + _dot(lo, m)


def _rmsnorm(x, w):
    return x * lax.rsqrt(jnp.mean(x * x, axis=-1, keepdims=True) + EPS) * w


def _proj_kernel(x_ref, nw_ref, w_ref, g_ref, qw_ref, kw_ref, cos_ref, sin_ref,
                 qa_ref, ka_ref, va_ref, qs_ref, ks_ref, vs_ref):
    dg = qa_ref.shape[-1]
    h = _rmsnorm(x_ref[...], nw_ref[...]).astype(BF16)
    lane = lax.broadcasted_iota(jnp.int32, (1, LANES), 1)
    first_half = (lane % HEAD_DIM) < HALF
    cos = cos_ref[...]
    sin = sin_ref[...]

    def head_norm_rope(y, w_full):
        outs = []
        for j in range(dg // MXU_DIM):
            yc = y[:, j * MXU_DIM:(j + 1) * MXU_DIM]
            ss = _split_dot(yc * yc, g_ref[...])
            yn = yc * lax.rsqrt(ss * (1.0 / HEAD_DIM) + EPS) * w_full[:, j * MXU_DIM:(j + 1) * MXU_DIM]
            for i in range(MXU_DIM // LANES):
                v = yn[:, i * LANES:(i + 1) * LANES]
                partner = jnp.where(first_half, pltpu.roll(v, LANES - HALF, 1), pltpu.roll(v, HALF, 1))
                outs.append(v * cos + partner * sin)
        return jnp.concatenate(outs, axis=1)

    def proj(n):
        return _dot(h, w_ref[:, n * dg:(n + 1) * dg])

    scale = HEAD_DIM ** -0.5
    qa_ref[...] = (head_norm_rope(proj(0), qw_ref[...]) * scale).astype(BF16)
    ka_ref[...] = head_norm_rope(proj(1), kw_ref[...]).astype(BF16)
    va_ref[...] = proj(2).astype(BF16)
    qs_ref[...] = (proj(3) * scale).astype(BF16)
    ks_ref[...] = proj(4).astype(BF16)
    vs_ref[...] = proj(5).astype(BF16)


def _proj_call(x2, attn_norm_w, w_in, q_norm_w, k_norm_w, seq):
    t, d = x2.shape
    dg = w_in.shape[1] // 6
    tm = 512
    n_heads = dg // HEAD_DIM
    pos = jnp.arange(seq, dtype=F32)
    inv_freq = ROPE_THETA ** (-jnp.arange(0, HEAD_DIM, 2, dtype=F32) / HEAD_DIM)
    ang = pos[:, None] * inv_freq[None, :]
    cos_t = jnp.tile(jnp.cos(ang), (1, LANES // HALF))
    sin_t = jnp.tile(jnp.concatenate([-jnp.sin(ang), jnp.sin(ang)], axis=1), (1, LANES // HEAD_DIM))
    hid = jnp.arange(MXU_DIM) // HEAD_DIM
    g = (hid[:, None] == hid[None, :]).astype(BF16)
    qw = jnp.tile(q_norm_w.reshape(1, HEAD_DIM), (1, n_heads))
    kw = jnp.tile(k_norm_w.reshape(1, HEAD_DIM), (1, n_heads))
    tiles_per_seq = seq // tm
    row = lambda i: (i, 0)
    const = lambda i: (0, 0)
    out = jax.ShapeDtypeStruct((t, dg), BF16)
    return pl.pallas_call(
        _proj_kernel,
        grid=(t // tm,),
        in_specs=[
            pl.BlockSpec((tm, d), row),
            pl.BlockSpec((1, d), const),
            pl.BlockSpec(w_in.shape, const),
            pl.BlockSpec(g.shape, const),
            pl.BlockSpec((1, dg), const),
            pl.BlockSpec((1, dg), const),
            pl.BlockSpec((tm, LANES), lambda i: (i % tiles_per_seq, 0)),
            pl.BlockSpec((tm, LANES), lambda i: (i % tiles_per_seq, 0)),
        ],
        out_specs=[pl.BlockSpec((tm, dg), row)] * 6,
        out_shape=[out] * 6,
        compiler_params=pltpu.CompilerParams(
            dimension_semantics=("arbitrary",), vmem_limit_bytes=VMEM_LIMIT),
        name="proj",
    )(x2, attn_norm_w.reshape(1, d), w_in.astype(BF16), g, qw, kw, cos_t, sin_t)


def _dilated_kernel(q_ref, k_ref, v_ref, o_ref, f_ref, f4_ref, q4_ref, k4_ref, v4_ref,
                    q16_ref, k16_ref, v16_ref, bias_ref, *state_refs):
    sa_refs, sb_refs = state_refs[:3], state_refs[3:]
    s = q_ref.shape[1]
    quarter, sixteenth = s // 4, s // 16

    for src, d4, d16 in ((q_ref, q4_ref, q16_ref), (k_ref, k4_ref, k16_ref), (v_ref, v4_ref, v16_ref)):
        f_ref[...] = src[0].astype(F32)
        for c in range(4):
            chunk = f_ref[pl.ds(c, quarter, stride=4), :]
            f4_ref[pl.ds(c * quarter, quarter), :] = chunk
            d4[pl.ds(c * quarter, quarter), :] = chunk.astype(BF16)
        for c in range(16):
            chunk = f4_ref[pl.ds((c % 4) * quarter + c // 4, sixteenth, stride=4), :]
            d16[pl.ds(c * sixteenth, sixteenth), :] = chunk.astype(BF16)

    lane = lax.broadcasted_iota(jnp.int32, (1, LANES), 1)
    head0 = lane < HEAD_DIM
    qi = lax.broadcasted_iota(jnp.int32, (2 * BLOCK, 2 * BLOCK), 0) % BLOCK
    kj = lax.broadcasted_iota(jnp.int32, (2 * BLOCK, 2 * BLOCK), 1)
    dist = qi + BLOCK - kj
    bias_ref[...] = jnp.where((dist >= 0) & (dist <= N_BACK), 0.0, -jnp.inf)

    def attend(load_q, k_src, v_src, blocks):
        qs, keys, firsts = [], [], []
        for start, first in blocks:
            q = load_q(pl.ds(start, BLOCK))
            zero = jnp.zeros_like(q)
            qs.append(jnp.concatenate([jnp.where(head0, q, zero), jnp.where(head0, zero, q)], axis=0))
            keys.append(pl.ds(start, BLOCK) if first else pl.ds(_aligned(start - BLOCK), 2 * BLOCK))
            firsts.append(first)
        n = len(qs)
        score = lambda j: (_dot_nt(qs[j], k_src[keys[j], :]) * LOG2E
                           + (bias_ref[:, BLOCK:] if firsts[j] else bias_ref[...]))
        ss = [score(j) for j in range(min(2, n))]
        ms, dens, pvs = [], [], []
        for j in range(n):
            ms.append(jnp.max(ss[j], axis=-1, keepdims=True))
            p = jnp.exp2(ss[j] - ms[j])
            dens.append(jnp.sum(p, axis=-1, keepdims=True))
            if j + 2 < n:
                ss.append(score(j + 2))
            pvs.append(_dot(p.astype(BF16), v_src[keys[j], :]))
        unstack = lambda a: jnp.where(head0, a[:BLOCK], a[BLOCK:])
        return [tuple(unstack(a) for a in t) for t in zip(pvs, ms, dens)]

    def fold(state, new):
        (acc_p, m_p, den_p), (acc, m, den) = state, new
        mx = jnp.maximum(m_p, m)
        w_p = jnp.exp2(m_p - mx)
        w_n = jnp.exp2(m - mx)
        return acc_p * w_p + acc * w_n, mx, den_p * w_p + den * w_n

    def sweep(per_seq, group):
        total = s // BLOCK
        if per_seq <= 2 * DIL_GROUP:
            n = per_seq * max(1, DIL_GROUP // per_seq)
            flags = ((True,) + (False,) * (per_seq - 1)) * (n // per_seq)
            lo = 0
        else:
            n = DIL_GROUP
            flags = (False,) * n
            group(0, (True,) + flags[1:])
            lo = 1

        def body(t, carry):
            for j in range(0, n, DIL_GROUP):
                group(t * (n * BLOCK) + j * BLOCK, flags[j:j + DIL_GROUP])
            return carry
        lax.fori_loop(lo, total // n, body, 0)

    def rows_of(base, n):
        return [_aligned(base + j * BLOCK) for j in range(n)]

    def group16(base, flags):
        st = rows_of(base, len(flags))
        res = attend(lambda r: q16_ref[r, :], k16_ref, v16_ref, list(zip(st, flags)))
        for start, new in zip(st, res):
            for ref, val in zip(sa_refs, new):
                ref[pl.ds(start, BLOCK), :] = val
    sweep(sixteenth // BLOCK, group16)
    for c in range(16):
        dst = pl.ds((c % 4) * quarter + c // 4, sixteenth, stride=4)
        for a_ref, b_ref in zip(sa_refs, sb_refs):
            b_ref[dst, :] = a_ref[pl.ds(c * sixteenth, sixteenth), :]

    def group4(base, flags):
        st = rows_of(base, len(flags))
        res = attend(lambda r: q4_ref[r, :], k4_ref, v4_ref, list(zip(st, flags)))
        for start, new in zip(st, res):
            rows = pl.ds(start, BLOCK)
            for ref, val in zip(sb_refs, fold([ref[rows, :] for ref in sb_refs], new)):
                ref[rows, :] = val
    sweep(quarter // BLOCK, group4)
    for c in range(4):
        for a_ref, b_ref in zip(sa_refs, sb_refs):
            a_ref[pl.ds(c, quarter, stride=4), :] = b_ref[pl.ds(c * quarter, quarter), :]

    def group1(base, flags):
        st = rows_of(base, len(flags))
        res = attend(lambda r: q_ref[0, r, :], k_ref.at[0], v_ref.at[0], list(zip(st, flags)))
        for start, new in zip(st, res):
            rows = pl.ds(start, BLOCK)
            acc, _, den = fold([ref[rows, :] for ref in sa_refs], new)
            o_ref[0, rows, :] = acc / den
    sweep(s // BLOCK, group1)


def _aligned(i):
    return i if isinstance(i, int) else pl.multiple_of(i, BLOCK)


def _dilated_call(qa, ka, va):
    b, s, dg = qa.shape
    pairs = dg // LANES
    spec = pl.BlockSpec((1, s, LANES), lambda bi, hp: (bi, 0, hp))
    slab = lambda dt: pltpu.VMEM((s, LANES), dt)
    return pl.pallas_call(
        _dilated_kernel,
        grid=(b, pairs),
        in_specs=[spec] * 3,
        out_specs=spec,
        out_shape=jax.ShapeDtypeStruct((b, s, dg), F32),
        scratch_shapes=([slab(F32)] * 2 + [slab(BF16)] * 6 + [pltpu.VMEM((2 * BLOCK, 2 * BLOCK), F32)]
                        + [slab(F32)] * 6),
        compiler_params=pltpu.CompilerParams(
            dimension_semantics=("arbitrary",) * 2, vmem_limit_bytes=VMEM_LIMIT),
        name="dilated",
    )(qa, ka, va)


def _sb_kernel(q_ref, k_ref, v_ref, u_ref, o_ref, kp_ref, vp_ref):
    s = q_ref.shape[1]
    kp_ref[pl.ds(0, SB_PAD), :] = jnp.zeros((SB_PAD, LANES), BF16)
    vp_ref[pl.ds(0, SB_PAD), :] = jnp.zeros((SB_PAD, LANES), BF16)
    kp_ref[pl.ds(SB_PAD, s), :] = k_ref[0]
    vp_ref[pl.ds(SB_PAD, s), :] = v_ref[0]

    lane = lax.broadcasted_iota(jnp.int32, (1, LANES), 1)
    head0 = lane < HEAD_DIM
    ri = lax.broadcasted_iota(jnp.int32, (2 * SB_Q, SB_K), 0) % SB_Q
    cj = lax.broadcasted_iota(jnp.int32, (2 * SB_Q, SB_K), 1)
    causal = cj < ri + (SB_K - SB_Q)
    u = u_ref[...]

    def neg_log_keep(z, mask):
        nk = jnp.maximum(z, 0.0) + jnp.log2(1.0 + jnp.exp2(-jnp.abs(z)))
        return nk if mask is None else jnp.where(mask, nk, 0.0)

    def inclusive_suffix(nk):
        hi = nk.astype(BF16)
        lo = (nk - hi.astype(F32)).astype(BF16)
        cs = _dot(jnp.concatenate([hi, lo], axis=0), u)
        return cs[:2 * SB_Q] + cs[2 * SB_Q:]

    def weights(z, cs, shift, mask):
        x = z - cs
        if shift is not None:
            x = x - shift
        a = jnp.exp2(x)
        if mask is not None:
            a = jnp.where(mask, a, 0.0)
        return a.astype(BF16)

    def sweep(qs, starts, shifts, mask):
        keys = [pl.ds(pl.multiple_of(st, SB_Q), SB_K) for st in starts]
        zs = [_dot_nt(q, kp_ref[k, :]) * LOG2E for q, k in zip(qs, keys)]
        nks = [neg_log_keep(z, mask) for z in zs]
        css = [inclusive_suffix(nk) for nk in nks]
        ws = [weights(z, cs, sh, mask) for z, cs, sh in zip(zs, css, shifts)]
        pvs = [_dot(w, vp_ref[k, :]) for w, k in zip(ws, keys)]
        tots = [jnp.sum(nk, axis=-1, keepdims=True) for nk in nks]
        return pvs, tots

    def live(shifts):
        return jnp.min(functools.reduce(jnp.minimum, shifts)) < SB_DEAD

    def group_body(p, carry):
        qbs = [SB_GROUP * p + j for j in range(SB_GROUP)]
        qs = []
        for qb in qbs:
            q = q_ref[0, pl.ds(pl.multiple_of(qb * SB_Q, SB_Q), SB_Q), :]
            zero = jnp.zeros_like(q)
            qs.append(jnp.concatenate([jnp.where(head0, q, zero), jnp.where(head0, zero, q)], axis=0))
        pad_blocks = SB_PAD // SB_Q
        accs, shifts = sweep(qs, [(qb - 1 + pad_blocks) * SB_Q for qb in qbs], [None] * SB_GROUP, causal)

        def cond(state):
            return (state[0] <= (SB_GROUP * p + SB_GROUP - 1) // 2) & state[1]

        def body(state):
            i, _, accs, shifts = state
            pvs, tots = sweep(qs, [(qb - 1 + pad_blocks - 2 * i) * SB_Q for qb in qbs], shifts, None)
            accs = tuple(a + pv for a, pv in zip(accs, pvs))
            shifts = tuple(sh + t for sh, t in zip(shifts, tots))
            return i + 1, live(shifts), accs, shifts

        state = (jnp.int32(1), live(shifts), tuple(accs), tuple(shifts))
        accs = lax.while_loop(cond, body, state)[2]
        for qb, acc in zip(qbs, accs):
            rows = pl.ds(pl.multiple_of(qb * SB_Q, SB_Q), SB_Q)
            o_ref[0, rows, :] = jnp.where(head0, acc[:SB_Q], acc[SB_Q:])
        return carry

    lax.fori_loop(0, s // (SB_GROUP * SB_Q), group_body, 0)


def _sb_call(qs, ks, vs):
    b, s, dg = qs.shape
    pairs = dg // LANES
    idx = jnp.arange(SB_K)
    u = (idx[:, None] >= idx[None, :]).astype(BF16)
    spec = pl.BlockSpec((1, s, LANES), lambda bi, hp: (bi, 0, hp))
    return pl.pallas_call(
        _sb_kernel,
        grid=(b, pairs),
        in_specs=[spec, spec, spec, pl.BlockSpec((SB_K, SB_K), lambda bi, hp: (0, 0))],
        out_specs=spec,
        out_shape=jax.ShapeDtypeStruct((b, s, dg), F32),
        scratch_shapes=[pltpu.VMEM((s + SB_PAD, LANES), BF16)] * 2,
        compiler_params=pltpu.CompilerParams(
            dimension_semantics=("arbitrary",) * 2, vmem_limit_bytes=VMEM_LIMIT),
        name="stick_breaking",
    )(qs, ks, vs, u)


def _out_ffn_kernel(x_ref, od_ref, os_ref, dw_ref, sw_ref, wo_ref, fw_ref, wg_ref, wu_ref, wd_ref,
                    out_ref, x1_ref, act_ref, *, f_chunk):
    dg = od_ref.shape[-1]
    nd = _rmsnorm(od_ref[...], dw_ref[...]).astype(BF16)
    ns = _rmsnorm(os_ref[...], sw_ref[...]).astype(BF16)
    x1 = x_ref[...] + _dot(nd, wo_ref[:dg, :]) + _dot(ns, wo_ref[dg:, :])
    h = _rmsnorm(x1, fw_ref[...]).astype(BF16)
    x1_ref[...] = x1

    def gate_up(i):
        cols = slice(i * f_chunk, (i + 1) * f_chunk)
        return _dot(h, wg_ref[:, cols]), _dot(h, wu_ref[:, cols])

    n_chunks = wg_ref.shape[1] // f_chunk
    nxt = gate_up(0)
    for i in range(n_chunks):
        g, up = nxt
        if i + 1 < n_chunks:
            nxt = gate_up(i + 1)
        act_ref[:, i * f_chunk:(i + 1) * f_chunk] = (g / (1.0 + jnp.exp(-g)) * up).astype(BF16)
    out_ref[...] = x1_ref[...] + _dot(act_ref[...], wd_ref[...])


def _out_ffn_call(x2, o_dil, o_sb, dil_w, sb_w, w_out, ffn_w, w_gate, w_up, w_down):
    t, d = x2.shape
    dg = o_dil.shape[1]
    d_ff = w_gate.shape[1]
    tm = 512
    f_chunk = 256
    row = lambda i: (i, 0)
    const = lambda i: (0, 0)
    resident = lambda shape: pl.BlockSpec(shape, const, pipeline_mode=pl.Buffered(1))
    return pl.pallas_call(
        functools.partial(_out_ffn_kernel, f_chunk=f_chunk),
        grid=(t // tm,),
        in_specs=[
            pl.BlockSpec((tm, d), row),
            pl.BlockSpec((tm, dg), row),
            pl.BlockSpec((tm, dg), row),
            pl.BlockSpec((1, dg), const),
            pl.BlockSpec((1, dg), const),
            resident((2 * dg, d)),
            pl.BlockSpec((1, d), const),
            resident((d, d_ff)),
            resident((d, d_ff)),
            resident((d_ff, d)),
        ],
        out_specs=pl.BlockSpec((tm, d), row),
        out_shape=jax.ShapeDtypeStruct((t, d), F32),
        scratch_shapes=[pltpu.VMEM((tm, d), F32), pltpu.VMEM((tm, d_ff), BF16)],
        compiler_params=pltpu.CompilerParams(
            dimension_semantics=("arbitrary",), vmem_limit_bytes=VMEM_LIMIT),
        name="out_ffn",
    )(x2, o_dil, o_sb, dil_w.reshape(1, dg), sb_w.reshape(1, dg), w_out.astype(BF16),
      ffn_w.reshape(1, d), w_gate.astype(BF16), w_up.astype(BF16), w_down.astype(BF16))


def kernel(x, attn_norm_w, w_in, q_norm_w, k_norm_w, dil_out_norm_w, sb_out_norm_w, w_out,
           ffn_norm_w, w_gate, w_up, w_down):
    b, s, d = x.shape
    for l in range(attn_norm_w.shape[0]):
        x2 = x.reshape(b * s, d)
        qa, ka, va, qs, ks, vs = _proj_call(x2, attn_norm_w[l], w_in[l], q_norm_w[l], k_norm_w[l], s)
        dg = qa.shape[1]
        qa, ka, va, qs, ks, vs = (a.reshape(b, s, dg) for a in (qa, ka, va, qs, ks, vs))

        o_dil = _dilated_call(qa, ka, va)
        o_sb = _sb_call(qs, ks, vs)

        x = _out_ffn_call(x2, o_dil.reshape(b * s, dg), o_sb.reshape(b * s, dg), dil_out_norm_w[l],
                          sb_out_norm_w[l], w_out[l], ffn_norm_w[l], w_gate[l], w_up[l],
                          w_down[l]).reshape(b, s, d)
    return x
```

```python
import functools

import jax
import jax.numpy as jnp
from jax import lax
from jax.experimental import pallas as pl
from jax.experimental.pallas import tpu as pltpu

F32 = jnp.float32
BF16 = jnp.bfloat16

HEAD_DIM = 64
HALF = HEAD_DIM // 2
LANES = 128
MXU_DIM = 256
N_BACK = 128
BLOCK = 128
DIL_GROUP = 4
SB_Q = 128
SB_K = MXU_DIM
SB_GROUP = 4
SB_PAD = SB_K - SB_Q
SB_DEAD = 127.0
LOG2E = 1.4426950408889634
ROPE_THETA = 10000.0
EPS = 1e-6
VMEM_LIMIT = 56 * 1024 * 1024


def _dot(a, b):
    return jnp.dot(a, b, preferred_element_type=F32)


def _dot_nt(a, b):
    return lax.dot_general(a, b, (((1,), (1,)), ((), ())), preferred_element_type=F32)


def _split_dot(x, m):
    hi = x.astype(BF16)
    lo = (x - hi.astype(F32)).astype(BF16)
    return _dot(hi, m) + _dot(lo, m)


def _rmsnorm(x, w):
    return x * lax.rsqrt(jnp.mean(x * x, axis=-1, keepdims=True) + EPS) * w


def _software_pipeline(n, first, middle, last):
    first(0, 0)
    if n > 1:
        first(1, 1)
    middle(0, 0)
    for g in range(1, n):
        if g + 1 < n:
            first(g + 1, (g + 1) % 2)
        last(g - 1, (g - 1) % 2)
        middle(g, g % 2)
    last(n - 1, (n - 1) % 2)


def _proj_kernel(x_ref, nw_ref, w_ref, g_ref, qw_ref, kw_ref, cos_ref, sin_ref,
                 qa_ref, ka_ref, va_ref, qs_ref, ks_ref, vs_ref):
    dg = qa_ref.shape[-1]
    h = _rmsnorm(x_ref[...], nw_ref[...]).astype(BF16)
    lane = lax.broadcasted_iota(jnp.int32, (1, LANES), 1)
    first_half = (lane % HEAD_DIM) < HALF
    cos = cos_ref[...]
    sin = sin_ref[...]

    def head_norm_rope(y, w_full):
        outs = []
        for j in range(dg // MXU_DIM):
            yc = y[:, j * MXU_DIM:(j + 1) * MXU_DIM]
            ss = _split_dot(yc * yc, g_ref[...])
            yn = yc * lax.rsqrt(ss * (1.0 / HEAD_DIM) + EPS) * w_full[:, j * MXU_DIM:(j + 1) * MXU_DIM]
            for i in range(MXU_DIM // LANES):
                v = yn[:, i * LANES:(i + 1) * LANES]
                partner = jnp.where(first_half, pltpu.roll(v, LANES - HALF, 1), pltpu.roll(v, HALF, 1))
                outs.append(v * cos + partner * sin)
        return jnp.concatenate(outs, axis=1)

    def proj(n):
        return _dot(h, w_ref[:, n * dg:(n + 1) * dg])

    scale = HEAD_DIM ** -0.5
    qa_ref[...] = (head_norm_rope(proj(0), qw_ref[...]) * scale).astype(BF16)
    ka_ref[...] = head_norm_rope(proj(1), kw_ref[...]).astype(BF16)
    va_ref[...] = proj(2).astype(BF16)
    qs_ref[...] = (proj(3) * scale).astype(BF16)
    ks_ref[...] = proj(4).astype(BF16)
    vs_ref[...] = proj(5).astype(BF16)


def _proj_call(x2, attn_norm_w, w_in, q_norm_w, k_norm_w, seq):
    t, d = x2.shape
    dg = w_in.shape[1] // 6
    tm = 512
    n_heads = dg // HEAD_DIM
    pos = jnp.arange(seq, dtype=F32)
    inv_freq = ROPE_THETA ** (-jnp.arange(0, HEAD_DIM, 2, dtype=F32) / HEAD_DIM)
    ang = pos[:, None] * inv_freq[None, :]
    cos_t = jnp.tile(jnp.cos(ang), (1, LANES // HALF))
    sin_t = jnp.tile(jnp.concatenate([-jnp.sin(ang), jnp.sin(ang)], axis=1), (1, LANES // HEAD_DIM))
    hid = jnp.arange(MXU_DIM) // HEAD_DIM
    g = (hid[:, None] == hid[None, :]).astype(BF16)
    qw = jnp.tile(q_norm_w.reshape(1, HEAD_DIM), (1, n_heads))
    kw = jnp.tile(k_norm_w.reshape(1, HEAD_DIM), (1, n_heads))
    tiles_per_seq = seq // tm
    row = lambda i: (i, 0)
    const = lambda i: (0, 0)
    out = jax.ShapeDtypeStruct((t, dg), BF16)
    return pl.pallas_call(
        _proj_kernel,
        grid=(t // tm,),
        in_specs=[
            pl.BlockSpec((tm, d), row),
            pl.BlockSpec((1, d), const),
            pl.BlockSpec(w_in.shape, const),
            pl.BlockSpec(g.shape, const),
            pl.BlockSpec((1, dg), const),
            pl.BlockSpec((1, dg), const),
            pl.BlockSpec((tm, LANES), lambda i: (i % tiles_per_seq, 0)),
            pl.BlockSpec((tm, LANES), lambda i: (i % tiles_per_seq, 0)),
        ],
        out_specs=[pl.BlockSpec((tm, dg), row)] * 6,
        out_shape=[out] * 6,
        compiler_params=pltpu.CompilerParams(
            dimension_semantics=("arbitrary",), vmem_limit_bytes=VMEM_LIMIT),
        name="proj",
    )(x2, attn_norm_w.reshape(1, d), w_in.astype(BF16), g, qw, kw, cos_t, sin_t)


def _dilated_kernel(q_ref, k_ref, v_ref, o_ref, f_ref, f4_ref, q4_ref, k4_ref, v4_ref,
                    q16_ref, k16_ref, v16_ref, bias_ref, s_ref, p_ref, md_ref, *state_refs):
    sa_refs, sb_refs = state_refs[:3], state_refs[3:]
    s = q_ref.shape[1]
    quarter, sixteenth = s // 4, s // 16
    n_blocks = s // BLOCK

    for src, d4, d16 in ((q_ref, q4_ref, q16_ref), (k_ref, k4_ref, k16_ref), (v_ref, v4_ref, v16_ref)):
        f_ref[...] = src[0].astype(F32)
        for c in range(4):
            chunk = f_ref[pl.ds(c, quarter, stride=4), :]
            f4_ref[pl.ds(c * quarter, quarter), :] = chunk
            d4[pl.ds(c * quarter, quarter), :] = chunk.astype(BF16)
        for c in range(16):
            chunk = f4_ref[pl.ds((c % 4) * quarter + c // 4, sixteenth, stride=4), :]
            d16[pl.ds(c * sixteenth, sixteenth), :] = chunk.astype(BF16)

    lane = lax.broadcasted_iota(jnp.int32, (1, LANES), 1)
    head0 = lane < HEAD_DIM
    qi = lax.broadcasted_iota(jnp.int32, (2 * BLOCK, 2 * BLOCK), 0) % BLOCK
    kj = lax.broadcasted_iota(jnp.int32, (2 * BLOCK, 2 * BLOCK), 1)
    dist = qi + BLOCK - kj
    bias_ref[...] = jnp.where((dist >= 0) & (dist <= N_BACK), 0.0, -jnp.inf)

    def unstack(a):
        return jnp.where(head0, a[:BLOCK], a[BLOCK:])

    def fold(state, new):
        (acc_p, m_p, den_p), (acc, m, den) = state, new
        mx = jnp.maximum(m_p, m)
        w_p = jnp.exp2(m_p - mx)
        w_n = jnp.exp2(m - mx)
        return acc_p * w_p + acc * w_n, mx, den_p * w_p + den * w_n

    def keep16(rows, new):
        for ref, val in zip(sa_refs, new):
            ref[rows, :] = val

    def fold4(rows, new):
        for ref, val in zip(sb_refs, fold([ref[rows, :] for ref in sb_refs], new)):
            ref[rows, :] = val

    def finish1(rows, new):
        acc, _, den = fold([ref[rows, :] for ref in sa_refs], new)
        o_ref[0, rows, :] = acc / den

    nat = lambda ref: ref.at[0]
    branches = ((q16_ref, k16_ref, v16_ref, sixteenth // BLOCK, keep16),
                (q4_ref, k4_ref, v4_ref, quarter // BLOCK, fold4),
                (nat(q_ref), nat(k_ref), nat(v_ref), n_blocks, finish1))
    groups_per_branch = n_blocks // DIL_GROUP

    def blocks_of(g):
        branch = branches[g // groups_per_branch]
        out = []
        for j in range(DIL_GROUP):
            blk = (g % groups_per_branch) * DIL_GROUP + j
            start = blk * BLOCK
            first = blk % branch[3] == 0
            out.append((branch, start, (start, BLOCK) if first else (start - BLOCK, 2 * BLOCK), first))
        return out

    def stage_scores(g, slot):
        for j, ((q_src, k_src, _, _, _), start, (k0, kn), _) in enumerate(blocks_of(g)):
            q = q_src[pl.ds(start, BLOCK), :]
            zero = jnp.zeros_like(q)
            qs = jnp.concatenate([jnp.where(head0, q, zero), jnp.where(head0, zero, q)], axis=0)
            s_ref[slot, j, :, pl.ds(0, kn)] = _dot_nt(qs, k_src[pl.ds(k0, kn), :])

    def stage_softmax(g, slot):
        for j, (_, _, (_, kn), first) in enumerate(blocks_of(g)):
            bias = bias_ref[:, BLOCK:] if first else bias_ref[...]
            sc = s_ref[slot, j, :, pl.ds(0, kn)] * LOG2E + bias
            m = jnp.max(sc, axis=-1, keepdims=True)
            p = jnp.exp2(sc - m)
            p_ref[slot, j, :, pl.ds(0, kn)] = p.astype(BF16)
            md_ref[slot, j, 0] = unstack(m)
            md_ref[slot, j, 1] = unstack(jnp.sum(p, axis=-1, keepdims=True))

    def stage_values(g, slot):
        if g > 0 and g % groups_per_branch == 0:
            regroup(g // groups_per_branch)
        for j, ((_, _, v_src, _, post), start, (k0, kn), _) in enumerate(blocks_of(g)):
            acc = unstack(_dot(p_ref[slot, j, :, pl.ds(0, kn)], v_src[pl.ds(k0, kn), :]))
            post(pl.ds(start, BLOCK), (acc, md_ref[slot, j, 0], md_ref[slot, j, 1]))

    def regroup(next_branch):
        if next_branch == 1:
            for c in range(16):
                dst = pl.ds((c % 4) * quarter + c // 4, sixteenth, stride=4)
                for a_ref, b_ref in zip(sa_refs, sb_refs):
                    b_ref[dst, :] = a_ref[pl.ds(c * sixteenth, sixteenth), :]
        else:
            for c in range(4):
                for a_ref, b_ref in zip(sa_refs, sb_refs):
                    a_ref[pl.ds(c, quarter, stride=4), :] = b_ref[pl.ds(c * quarter, quarter), :]

    _software_pipeline(len(branches) * groups_per_branch, stage_scores, stage_softmax, stage_values)


def _dilated_call(qa, ka, va):
    b, s, dg = qa.shape
    pairs = dg // LANES
    spec = pl.BlockSpec((1, s, LANES), lambda bi, hp: (bi, 0, hp))
    slab = lambda dt: pltpu.VMEM((s, LANES), dt)
    tile = (2, DIL_GROUP, 2 * BLOCK, 2 * BLOCK)
    return pl.pallas_call(
        _dilated_kernel,
        grid=(b, pairs),
        in_specs=[spec] * 3,
        out_specs=spec,
        out_shape=jax.ShapeDtypeStruct((b, s, dg), F32),
        scratch_shapes=([slab(F32)] * 2 + [slab(BF16)] * 6 + [pltpu.VMEM((2 * BLOCK, 2 * BLOCK), F32),
                        pltpu.VMEM(tile, F32), pltpu.VMEM(tile, BF16),
                        pltpu.VMEM((2, DIL_GROUP, 2, BLOCK, LANES), F32)] + [slab(F32)] * 6),
        compiler_params=pltpu.CompilerParams(
            dimension_semantics=("arbitrary",) * 2, vmem_limit_bytes=VMEM_LIMIT),
        name="dilated",
    )(qa, ka, va)


def _sb_kernel(q_ref, k_ref, v_ref, u_ref, o_ref, kp_ref, vp_ref, z_ref, w_ref, shift_ref, live_ref):
    s = q_ref.shape[1]
    kp_ref[pl.ds(0, SB_PAD), :] = jnp.zeros((SB_PAD, LANES), BF16)
    vp_ref[pl.ds(0, SB_PAD), :] = jnp.zeros((SB_PAD, LANES), BF16)
    kp_ref[pl.ds(SB_PAD, s), :] = k_ref[0]
    vp_ref[pl.ds(SB_PAD, s), :] = v_ref[0]

    lane = lax.broadcasted_iota(jnp.int32, (1, LANES), 1)
    head0 = lane < HEAD_DIM
    ri = lax.broadcasted_iota(jnp.int32, (2 * SB_Q, SB_K), 0) % SB_Q
    cj = lax.broadcasted_iota(jnp.int32, (2 * SB_Q, SB_K), 1)
    causal = cj < ri + (SB_K - SB_Q)
    u = u_ref[...]

    def neg_log_keep(z, mask):
        nk = jnp.maximum(z, 0.0) + jnp.log2(1.0 + jnp.exp2(-jnp.abs(z)))
        return nk if mask is None else jnp.where(mask, nk, 0.0)

    def inclusive_suffix(nk):
        hi = nk.astype(BF16)
        lo = (nk - hi.astype(F32)).astype(BF16)
        cs = _dot(jnp.concatenate([hi, lo], axis=0), u)
        return cs[:2 * SB_Q] + cs[2 * SB_Q:]

    def weights(z, cs, shift, mask):
        x = z - cs
        if shift is not None:
            x = x - shift
        a = jnp.exp2(x)
        if mask is not None:
            a = jnp.where(mask, a, 0.0)
        return a.astype(BF16)

    def live(shift):
        return jnp.min(shift) < SB_DEAD

    def block_start(qb):
        return qb * SB_Q if isinstance(qb, int) else pl.multiple_of(qb * SB_Q, SB_Q)

    def stacked_q(qb):
        q = q_ref[0, pl.ds(block_start(qb), SB_Q), :]
        zero = jnp.zeros_like(q)
        return jnp.concatenate([jnp.where(head0, q, zero), jnp.where(head0, zero, q)], axis=0)

    def tile_keys(qb, i):
        return pl.ds(block_start(qb - 1 + SB_PAD // SB_Q - 2 * i), SB_K)

    def score(qb, i):
        return _dot_nt(stacked_q(qb), kp_ref[tile_keys(qb, i), :])

    def unstack(acc):
        return jnp.where(head0, acc[:SB_Q], acc[SB_Q:])

    def stage_scores(g, slot):
        for j in range(SB_GROUP):
            z_ref[slot, j] = score(SB_GROUP * g + j, 0)

    def stage_weights(g, slot):
        for j in range(SB_GROUP):
            qb = SB_GROUP * g + j
            z = z_ref[slot, j] * LOG2E
            nk = neg_log_keep(z, causal)
            w_ref[slot, j] = weights(z, inclusive_suffix(nk), None, causal)
            shift = jnp.sum(nk, axis=-1, keepdims=True)
            shift_ref[qb] = shift
            live_ref[qb] = live(shift).astype(jnp.int32)

    def stage_values(g, slot):
        for j in range(SB_GROUP):
            qb = SB_GROUP * g + j
            o_ref[0, pl.ds(qb * SB_Q, SB_Q), :] = unstack(_dot(w_ref[slot, j], vp_ref[tile_keys(qb, 0), :]))

    _software_pipeline(s // (SB_GROUP * SB_Q), stage_scores, stage_weights, stage_values)

    def tail(qb):
        def cond(state):
            return (state[0] <= qb // 2) & state[1]

        def body(state):
            i, _, acc, shift = state
            z = score(qb, i) * LOG2E
            nk = neg_log_keep(z, None)
            w = weights(z, inclusive_suffix(nk), shift, None)
            shift = shift + jnp.sum(nk, axis=-1, keepdims=True)
            return i + 1, live(shift), acc + _dot(w, vp_ref[tile_keys(qb, i), :]), shift

        state = (jnp.int32(1), jnp.bool_(True), jnp.zeros((2 * SB_Q, LANES), F32), shift_ref[qb])
        o_ref[0, pl.ds(block_start(qb), SB_Q), :] += unstack(lax.while_loop(cond, body, state)[2])

    def tail_body(qb, carry):
        @pl.when(live_ref[qb] != 0)
        def _():
            tail(qb)
        return carry
    lax.fori_loop(0, s // SB_Q, tail_body, 0)


def _sb_call(qs, ks, vs):
    b, s, dg = qs.shape
    pairs = dg // LANES
    idx = jnp.arange(SB_K)
    u = (idx[:, None] >= idx[None, :]).astype(BF16)
    spec = pl.BlockSpec((1, s, LANES), lambda bi, hp: (bi, 0, hp))
    return pl.pallas_call(
        _sb_kernel,
        grid=(b, pairs),
        in_specs=[spec, spec, spec, pl.BlockSpec((SB_K, SB_K), lambda bi, hp: (0, 0))],
        out_specs=spec,
        out_shape=jax.ShapeDtypeStruct((b, s, dg), F32),
        scratch_shapes=[pltpu.VMEM((s + SB_PAD, LANES), BF16)] * 2 + [
            pltpu.VMEM((2, SB_GROUP, 2 * SB_Q, SB_K), F32),
            pltpu.VMEM((2, SB_GROUP, 2 * SB_Q, SB_K), BF16),
            pltpu.VMEM((s // SB_Q, 2 * SB_Q, 1), F32),
            pltpu.SMEM((s // SB_Q,), jnp.int32)],
        compiler_params=pltpu.CompilerParams(
            dimension_semantics=("arbitrary",) * 2, vmem_limit_bytes=VMEM_LIMIT),
        name="stick_breaking",
    )(qs, ks, vs, u)


def _out_ffn_kernel(x_ref, od_ref, os_ref, dw_ref, sw_ref, wo_ref, fw_ref, wg_ref, wu_ref, wd_ref,
                    out_ref, x1_ref, act_ref, *, f_chunk):
    dg = od_ref.shape[-1]
    nd = _rmsnorm(od_ref[...], dw_ref[...]).astype(BF16)
    ns = _rmsnorm(os_ref[...], sw_ref[...]).astype(BF16)
    x1 = x_ref[...] + _dot(nd, wo_ref[:dg, :]) + _dot(ns, wo_ref[dg:, :])
    h = _rmsnorm(x1, fw_ref[...]).astype(BF16)
    x1_ref[...] = x1

    def gate_up(i):
        cols = slice(i * f_chunk, (i + 1) * f_chunk)
        return _dot(h, wg_ref[:, cols]), _dot(h, wu_ref[:, cols])

    n_chunks = wg_ref.shape[1] // f_chunk
    nxt = gate_up(0)
    for i in range(n_chunks):
        g, up = nxt
        if i + 1 < n_chunks:
            nxt = gate_up(i + 1)
        act_ref[:, i * f_chunk:(i + 1) * f_chunk] = (g / (1.0 + jnp.exp(-g)) * up).astype(BF16)
    out_ref[...] = x1_ref[...] + _dot(act_ref[...], wd_ref[...])


def _out_ffn_call(x2, o_dil, o_sb, dil_w, sb_w, w_out, ffn_w, w_gate, w_up, w_down):
    t, d = x2.shape
    dg = o_dil.shape[1]
    d_ff = w_gate.shape[1]
    tm = 512
    f_chunk = 256
    row = lambda i: (i, 0)
    const = lambda i: (0, 0)
    resident = lambda shape: pl.BlockSpec(shape, const, pipeline_mode=pl.Buffered(1))
    return pl.pallas_call(
        functools.partial(_out_ffn_kernel, f_chunk=f_chunk),
        grid=(t // tm,),
        in_specs=[
            pl.BlockSpec((tm, d), row),
            pl.BlockSpec((tm, dg), row),
            pl.BlockSpec((tm, dg), row),
            pl.BlockSpec((1, dg), const),
            pl.BlockSpec((1, dg), const),
            resident((2 * dg, d)),
            pl.BlockSpec((1, d), const),
            resident((d, d_ff)),
            resident((d, d_ff)),
            resident((d_ff, d)),
        ],
        out_specs=pl.BlockSpec((tm, d), row),
        out_shape=jax.ShapeDtypeStruct((t, d), F32),
        scratch_shapes=[pltpu.VMEM((tm, d), F32), pltpu.VMEM((tm, d_ff), BF16)],
        compiler_params=pltpu.CompilerParams(
            dimension_semantics=("arbitrary",), vmem_limit_bytes=VMEM_LIMIT),
        name="out_ffn",
    )(x2, o_dil, o_sb, dil_w.reshape(1, dg), sb_w.reshape(1, dg), w_out.astype(BF16),
      ffn_w.reshape(1, d), w_gate.astype(BF16), w_up.astype(BF16), w_down.astype(BF16))


def kernel(x, attn_norm_w, w_in, q_norm_w, k_norm_w, dil_out_norm_w, sb_out_norm_w, w_out,
           ffn_norm_w, w_gate, w_up, w_down):
    b, s, d = x.shape
    for l in range(attn_norm_w.shape[0]):
        x2 = x.reshape(b * s, d)
        qa, ka, va, qs, ks, vs = _proj_call(x2, attn_norm_w[l], w_in[l], q_norm_w[l], k_norm_w[l], s)
        dg = qa.shape[1]
        qa, ka, va, qs, ks, vs = (a.reshape(b, s, dg) for a in (qa, ka, va, qs, ks, vs))

        o_dil = _dilated_call(qa, ka, va)
        o_sb = _sb_call(qs, ks, vs)

        x = _out_ffn_call(x2, o_dil.reshape(b * s, dg), o_sb.reshape(b * s, dg), dil_out_norm_w[l],
                          sb_out_norm_w[l], w_out[l], ffn_norm_w[l], w_gate[l], w_up[l],
                          w_down[l]).reshape(b, s, d)
    return x
```

```python
import functools

import jax
import jax.numpy as jnp
from jax import lax
from jax.experimental import pallas as pl
from jax.experimental.pallas import tpu as pltpu

F32 = jnp.float32
BF16 = jnp.bfloat16

HEAD_DIM = 64
HALF = HEAD_DIM // 2
LANES = 128
MXU_DIM = 256
N_BACK = 128
BLOCK = 128
DIL_GROUP = 1
SB_Q = 128
SB_K = MXU_DIM
SB_PAD = SB_K - SB_Q
SB_DEAD = 127.0
LOG2E = 1.4426950408889634
ROPE_THETA = 10000.0
EPS = 1e-6
VMEM_LIMIT = 56 * 1024 * 1024


def _dot(a, b):
    return jnp.dot(a, b, preferred_element_type=F32)


def _dot_nt(a, b):
    return lax.dot_general(a, b, (((1,), (1,)), ((), ())), preferred_element_type=F32)


def _rmsnorm(x, w):
    return x * lax.rsqrt(jnp.mean(x * x, axis=-1, keepdims=True) + EPS) * w


def _software_pipeline(n, stages):
    order = [0] + list(range(len(stages) - 1, 0, -1))
    for t in range(n + len(stages) - 1):
        for k in order:
            if 0 <= t - k < n:
                stages[k](t - k)


def _proj_kernel(x_ref, nw_ref, w_ref, g_ref, qw_ref, kw_ref, cos_ref, sin_ref,
                 qa_ref, ka_ref, va_ref, qs_ref, ks_ref, vs_ref):
    dg = qa_ref.shape[-1]
    h = _rmsnorm(x_ref[...], nw_ref[...]).astype(BF16)
    lane = lax.broadcasted_iota(jnp.int32, (1, LANES), 1)
    first_half = (lane % HEAD_DIM) < HALF
    cos = cos_ref[...]
    sin = sin_ref[...]

    def head_norm_rope(y, w_full):
        outs = []
        for j in range(dg // MXU_DIM):
            yc = y[:, j * MXU_DIM:(j + 1) * MXU_DIM]
            ss = _dot((yc * yc).astype(BF16), g_ref[...])
            yn = yc * lax.rsqrt(ss * (1.0 / HEAD_DIM) + EPS) * w_full[:, j * MXU_DIM:(j + 1) * MXU_DIM]
            for i in range(MXU_DIM // LANES):
                v = yn[:, i * LANES:(i + 1) * LANES]
                partner = jnp.where(first_half, pltpu.roll(v, LANES - HALF, 1), pltpu.roll(v, HALF, 1))
                outs.append(v * cos + partner * sin)
        return jnp.concatenate(outs, axis=1)

    def proj(n):
        return _dot(h, w_ref[:, n * dg:(n + 1) * dg])

    scale = HEAD_DIM ** -0.5
    ys = [proj(0)]
    for n, (ref, post) in enumerate((
            (qa_ref, lambda y: head_norm_rope(y, qw_ref[...]) * scale),
            (ka_ref, lambda y: head_norm_rope(y, kw_ref[...])),
            (va_ref, lambda y: y),
            (qs_ref, lambda y: y * scale),
            (ks_ref, lambda y: y),
            (vs_ref, lambda y: y))):
        if n + 1 < 6:
            ys.append(proj(n + 1))
        ref[...] = post(ys[n]).astype(BF16)


def _proj_call(x2, attn_norm_w, w_in, q_norm_w, k_norm_w, seq):
    t, d = x2.shape
    dg = w_in.shape[1] // 6
    tm = 512
    n_heads = dg // HEAD_DIM
    pos = jnp.arange(seq, dtype=F32)
    inv_freq = ROPE_THETA ** (-jnp.arange(0, HEAD_DIM, 2, dtype=F32) / HEAD_DIM)
    ang = pos[:, None] * inv_freq[None, :]
    cos_t = jnp.tile(jnp.cos(ang), (1, LANES // HALF))
    sin_t = jnp.tile(jnp.concatenate([-jnp.sin(ang), jnp.sin(ang)], axis=1), (1, LANES // HEAD_DIM))
    hid = jnp.arange(MXU_DIM) // HEAD_DIM
    g = (hid[:, None] == hid[None, :]).astype(BF16)
    qw = jnp.tile(q_norm_w.reshape(1, HEAD_DIM), (1, n_heads))
    kw = jnp.tile(k_norm_w.reshape(1, HEAD_DIM), (1, n_heads))
    tiles_per_seq = seq // tm
    row = lambda i: (i, 0)
    const = lambda i: (0, 0)
    out = jax.ShapeDtypeStruct((t, dg), BF16)
    return pl.pallas_call(
        _proj_kernel,
        grid=(t // tm,),
        in_specs=[
            pl.BlockSpec((tm, d), row),
            pl.BlockSpec((1, d), const),
            pl.BlockSpec(w_in.shape, const),
            pl.BlockSpec(g.shape, const),
            pl.BlockSpec((1, dg), const),
            pl.BlockSpec((1, dg), const),
            pl.BlockSpec((tm, LANES), lambda i: (i % tiles_per_seq, 0)),
            pl.BlockSpec((tm, LANES), lambda i: (i % tiles_per_seq, 0)),
        ],
        out_specs=[pl.BlockSpec((tm, dg), row)] * 6,
        out_shape=[out] * 6,
        compiler_params=pltpu.CompilerParams(
            dimension_semantics=("arbitrary",), vmem_limit_bytes=VMEM_LIMIT),
        name="proj",
    )(x2, attn_norm_w.reshape(1, d), w_in.astype(BF16), g, qw, kw, cos_t, sin_t)


def _dilated_kernel(q_ref, k_ref, v_ref, o_ref, f_ref, f4_ref, q4_ref, k4_ref, v4_ref,
                    q16_ref, k16_ref, v16_ref, bias_ref, s_ref, p_ref, md_ref, *state_refs):
    sa_refs, sb_refs = state_refs[:3], state_refs[3:]
    s = q_ref.shape[1]
    quarter, sixteenth = s // 4, s // 16
    n_blocks = s // BLOCK

    for src, d4, d16 in ((q_ref, q4_ref, q16_ref), (k_ref, k4_ref, k16_ref), (v_ref, v4_ref, v16_ref)):
        f_ref[...] = src[0].astype(F32)
        for c in range(4):
            chunk = f_ref[pl.ds(c, quarter, stride=4), :]
            f4_ref[pl.ds(c * quarter, quarter), :] = chunk
            d4[pl.ds(c * quarter, quarter), :] = chunk.astype(BF16)
        for c in range(16):
            chunk = f4_ref[pl.ds((c % 4) * quarter + c // 4, sixteenth, stride=4), :]
            d16[pl.ds(c * sixteenth, sixteenth), :] = chunk.astype(BF16)

    lane = lax.broadcasted_iota(jnp.int32, (1, LANES), 1)
    head0 = lane < HEAD_DIM
    qi = lax.broadcasted_iota(jnp.int32, (2 * BLOCK, 2 * BLOCK), 0) % BLOCK
    kj = lax.broadcasted_iota(jnp.int32, (2 * BLOCK, 2 * BLOCK), 1)
    dist = qi + BLOCK - kj
    bias_ref[...] = jnp.where((dist >= 0) & (dist <= N_BACK), 0.0, -jnp.inf)

    def unstack(a):
        return jnp.where(head0, a[:BLOCK], a[BLOCK:])

    def fold(state, new):
        (acc_p, m_p, den_p), (acc, m, den) = state, new
        mx = jnp.maximum(m_p, m)
        w_p = jnp.exp2(m_p - mx)
        w_n = jnp.exp2(m - mx)
        return acc_p * w_p + acc * w_n, mx, den_p * w_p + den * w_n

    def keep16(rows, new):
        for ref, val in zip(sa_refs, new):
            ref[rows, :] = val

    def fold4(rows, new):
        for ref, val in zip(sb_refs, fold([ref[rows, :] for ref in sb_refs], new)):
            ref[rows, :] = val

    def finish1(rows, new):
        acc, _, den = fold([ref[rows, :] for ref in sa_refs], new)
        o_ref[0, rows, :] = acc / den

    nat = lambda ref: ref.at[0]
    branches = ((q16_ref, k16_ref, v16_ref, sixteenth // BLOCK, keep16),
                (q4_ref, k4_ref, v4_ref, quarter // BLOCK, fold4),
                (nat(q_ref), nat(k_ref), nat(v_ref), n_blocks, finish1))
    groups_per_branch = n_blocks // DIL_GROUP

    def blocks_of(g):
        branch = branches[g // groups_per_branch]
        out = []
        for j in range(DIL_GROUP):
            blk = (g % groups_per_branch) * DIL_GROUP + j
            start = blk * BLOCK
            first = blk % branch[3] == 0
            out.append((branch, start, (start, BLOCK) if first else (start - BLOCK, 2 * BLOCK), first))
        return out

    def stage_scores(g, slot):
        for j, ((q_src, k_src, _, _, _), start, (k0, kn), _) in enumerate(blocks_of(g)):
            q = q_src[pl.ds(start, BLOCK), :]
            zero = jnp.zeros_like(q)
            qs = jnp.concatenate([jnp.where(head0, q, zero), jnp.where(head0, zero, q)], axis=0)
            s_ref[slot, j, :, pl.ds(0, kn)] = _dot_nt(qs, k_src[pl.ds(k0, kn), :])

    def stage_softmax(g, slot):
        for j, (_, _, (_, kn), first) in enumerate(blocks_of(g)):
            bias = bias_ref[:, BLOCK:] if first else bias_ref[...]
            sc = s_ref[slot, j, :, pl.ds(0, kn)] * LOG2E + bias
            m = jnp.max(sc, axis=-1, keepdims=True)
            p = jnp.exp2(sc - m)
            p_ref[slot, j, :, pl.ds(0, kn)] = p.astype(BF16)
            md_ref[slot, j, 0] = unstack(m)
            md_ref[slot, j, 1] = unstack(jnp.sum(p, axis=-1, keepdims=True))

    def stage_values(g, slot):
        if g > 0 and g % groups_per_branch == 0:
            regroup(g // groups_per_branch)
        for j, ((_, _, v_src, _, post), start, (k0, kn), _) in enumerate(blocks_of(g)):
            acc = unstack(_dot(p_ref[slot, j, :, pl.ds(0, kn)], v_src[pl.ds(k0, kn), :]))
            post(pl.ds(start, BLOCK), (acc, md_ref[slot, j, 0], md_ref[slot, j, 1]))

    def regroup(next_branch):
        if next_branch == 1:
            for c in range(16):
                dst = pl.ds((c % 4) * quarter + c // 4, sixteenth, stride=4)
                for a_ref, b_ref in zip(sa_refs, sb_refs):
                    b_ref[dst, :] = a_ref[pl.ds(c * sixteenth, sixteenth), :]
        else:
            for c in range(4):
                for a_ref, b_ref in zip(sa_refs, sb_refs):
                    a_ref[pl.ds(c, quarter, stride=4), :] = b_ref[pl.ds(c * quarter, quarter), :]

    _software_pipeline(len(branches) * groups_per_branch,
                       [lambda g, stage=stage: stage(g, g % 2)
                        for stage in (stage_scores, stage_softmax, stage_values)])


def _dilated_call(qa, ka, va):
    b, s, dg = qa.shape
    pairs = dg // LANES
    spec = pl.BlockSpec((1, s, LANES), lambda bi, hp: (bi, 0, hp))
    slab = lambda dt: pltpu.VMEM((s, LANES), dt)
    tile = (2, DIL_GROUP, 2 * BLOCK, 2 * BLOCK)
    return pl.pallas_call(
        _dilated_kernel,
        grid=(b, pairs),
        in_specs=[spec] * 3,
        out_specs=spec,
        out_shape=jax.ShapeDtypeStruct((b, s, dg), F32),
        scratch_shapes=([slab(F32)] * 2 + [slab(BF16)] * 6 + [pltpu.VMEM((2 * BLOCK, 2 * BLOCK), F32),
                        pltpu.VMEM(tile, F32), pltpu.VMEM(tile, BF16),
                        pltpu.VMEM((2, DIL_GROUP, 2, BLOCK, LANES), F32)] + [slab(F32)] * 6),
        compiler_params=pltpu.CompilerParams(
            dimension_semantics=("arbitrary",) * 2, vmem_limit_bytes=VMEM_LIMIT),
        name="dilated",
    )(qa, ka, va)


def _sb_kernel(q_ref, k_ref, v_ref, u_ref, o_ref, kp_ref, vp_ref, z_ref, w_ref, shift_ref, live_ref):
    s = q_ref.shape[1]
    kp_ref[pl.ds(0, SB_PAD), :] = jnp.zeros((SB_PAD, LANES), BF16)
    vp_ref[pl.ds(0, SB_PAD), :] = jnp.zeros((SB_PAD, LANES), BF16)
    kp_ref[pl.ds(SB_PAD, s), :] = k_ref[0]
    vp_ref[pl.ds(SB_PAD, s), :] = v_ref[0]

    lane = lax.broadcasted_iota(jnp.int32, (1, LANES), 1)
    head0 = lane < HEAD_DIM
    ri = lax.broadcasted_iota(jnp.int32, (2 * SB_Q, SB_K), 0) % SB_Q
    cj = lax.broadcasted_iota(jnp.int32, (2 * SB_Q, SB_K), 1)
    causal = cj < ri + (SB_K - SB_Q)
    u = u_ref[...]

    def neg_log_keep(z, mask):
        nk = jnp.maximum(z, 0.0) + jnp.log2(1.0 + jnp.exp2(-jnp.abs(z)))
        return nk if mask is None else jnp.where(mask, nk, 0.0)

    def inclusive_suffix(nk):
        hi = nk.astype(BF16)
        lo = (nk - hi.astype(F32)).astype(BF16)
        cs = _dot(jnp.concatenate([hi, lo], axis=0), u)
        return cs[:2 * SB_Q] + cs[2 * SB_Q:]

    def weights(z, cs, shift, mask):
        x = z - cs
        if shift is not None:
            x = x - shift
        a = jnp.exp2(x)
        if mask is not None:
            a = jnp.where(mask, a, 0.0)
        return a.astype(BF16)

    def live(shift):
        return jnp.min(shift) < SB_DEAD

    def block_start(qb):
        return qb * SB_Q if isinstance(qb, int) else pl.multiple_of(qb * SB_Q, SB_Q)

    def stacked_q(qb):
        q = q_ref[0, pl.ds(block_start(qb), SB_Q), :]
        zero = jnp.zeros_like(q)
        return jnp.concatenate([jnp.where(head0, q, zero), jnp.where(head0, zero, q)], axis=0)

    def tile_keys(qb, i):
        return pl.ds(block_start(qb - 1 + SB_PAD // SB_Q - 2 * i), SB_K)

    def score(qb, i):
        return _dot_nt(stacked_q(qb), kp_ref[tile_keys(qb, i), :])

    def unstack(acc):
        return jnp.where(head0, acc[:SB_Q], acc[SB_Q:])

    def stage_scores(qb):
        z_ref[qb % 2] = score(qb, 0)

    def stage_weights(qb):
        z = z_ref[qb % 2] * LOG2E
        nk = neg_log_keep(z, causal)
        w_ref[qb % 2] = weights(z, inclusive_suffix(nk), None, causal)
        shift = jnp.sum(nk, axis=-1, keepdims=True)
        shift_ref[qb] = shift
        live_ref[qb] = live(shift).astype(jnp.int32)

    def stage_values(qb):
        o_ref[0, pl.ds(qb * SB_Q, SB_Q), :] = unstack(_dot(w_ref[qb % 2], vp_ref[tile_keys(qb, 0), :]))

    _software_pipeline(s // SB_Q, [stage_scores, stage_weights, stage_values])

    def tail(qb):
        def cond(state):
            return (state[0] <= qb // 2) & state[1]

        def body(state):
            i, _, acc, shift = state
            z = score(qb, i) * LOG2E
            nk = neg_log_keep(z, None)
            w = weights(z, inclusive_suffix(nk), shift, None)
            shift = shift + jnp.sum(nk, axis=-1, keepdims=True)
            return i + 1, live(shift), acc + _dot(w, vp_ref[tile_keys(qb, i), :]), shift

        state = (jnp.int32(1), jnp.bool_(True), jnp.zeros((2 * SB_Q, LANES), F32), shift_ref[qb])
        o_ref[0, pl.ds(block_start(qb), SB_Q), :] += unstack(lax.while_loop(cond, body, state)[2])

    def tail_body(qb, carry):
        @pl.when(live_ref[qb] != 0)
        def _():
            tail(qb)
        return carry
    lax.fori_loop(0, s // SB_Q, tail_body, 0)


def _sb_call(qs, ks, vs):
    b, s, dg = qs.shape
    pairs = dg // LANES
    idx = jnp.arange(SB_K)
    u = (idx[:, None] >= idx[None, :]).astype(BF16)
    spec = pl.BlockSpec((1, s, LANES), lambda bi, hp: (bi, 0, hp))
    return pl.pallas_call(
        _sb_kernel,
        grid=(b, pairs),
        in_specs=[spec, spec, spec, pl.BlockSpec((SB_K, SB_K), lambda bi, hp: (0, 0))],
        out_specs=spec,
        out_shape=jax.ShapeDtypeStruct((b, s, dg), F32),
        scratch_shapes=[pltpu.VMEM((s + SB_PAD, LANES), BF16)] * 2 + [
            pltpu.VMEM((2, 2 * SB_Q, SB_K), F32),
            pltpu.VMEM((2, 2 * SB_Q, SB_K), BF16),
            pltpu.VMEM((s // SB_Q, 2 * SB_Q, 1), F32),
            pltpu.SMEM((s // SB_Q,), jnp.int32)],
        compiler_params=pltpu.CompilerParams(
            dimension_semantics=("arbitrary",) * 2, vmem_limit_bytes=VMEM_LIMIT),
        name="stick_breaking",
    )(qs, ks, vs, u)


def _out_ffn_kernel(x_ref, od_ref, os_ref, dw_ref, sw_ref, wo_ref, fw_ref, wg_ref, wu_ref, wd_ref,
                    out_ref, x1_ref, act_ref, *, f_chunk):
    dg = od_ref.shape[-1]
    nd = _rmsnorm(od_ref[...], dw_ref[...]).astype(BF16)
    ns = _rmsnorm(os_ref[...], sw_ref[...]).astype(BF16)
    x1 = x_ref[...] + _dot(nd, wo_ref[:dg, :]) + _dot(ns, wo_ref[dg:, :])
    h = _rmsnorm(x1, fw_ref[...]).astype(BF16)
    x1_ref[...] = x1

    def gate_up(i):
        cols = slice(i * f_chunk, (i + 1) * f_chunk)
        return _dot(h, wg_ref[:, cols]), _dot(h, wu_ref[:, cols])

    n_chunks = wg_ref.shape[1] // f_chunk
    nxt = gate_up(0)
    for i in range(n_chunks):
        g, up = nxt
        if i + 1 < n_chunks:
            nxt = gate_up(i + 1)
        act_ref[:, i * f_chunk:(i + 1) * f_chunk] = (g / (1.0 + jnp.exp(-g)) * up).astype(BF16)
    out_ref[...] = x1_ref[...] + _dot(act_ref[...], wd_ref[...])


def _out_ffn_call(x2, o_dil, o_sb, dil_w, sb_w, w_out, ffn_w, w_gate, w_up, w_down):
    t, d = x2.shape
    dg = o_dil.shape[1]
    d_ff = w_gate.shape[1]
    tm = 512
    f_chunk = 256
    row = lambda i: (i, 0)
    const = lambda i: (0, 0)
    resident = lambda shape: pl.BlockSpec(shape, const, pipeline_mode=pl.Buffered(1))
    return pl.pallas_call(
        functools.partial(_out_ffn_kernel, f_chunk=f_chunk),
        grid=(t // tm,),
        in_specs=[
            pl.BlockSpec((tm, d), row),
            pl.BlockSpec((tm, dg), row),
            pl.BlockSpec((tm, dg), row),
            pl.BlockSpec((1, dg), const),
            pl.BlockSpec((1, dg), const),
            resident((2 * dg, d)),
            pl.BlockSpec((1, d), const),
            resident((d, d_ff)),
            resident((d, d_ff)),
            resident((d_ff, d)),
        ],
        out_specs=pl.BlockSpec((tm, d), row),
        out_shape=jax.ShapeDtypeStruct((t, d), F32),
        scratch_shapes=[pltpu.VMEM((tm, d), F32), pltpu.VMEM((tm, d_ff), BF16)],
        compiler_params=pltpu.CompilerParams(
            dimension_semantics=("arbitrary",), vmem_limit_bytes=VMEM_LIMIT),
        name="out_ffn",
    )(x2, o_dil, o_sb, dil_w.reshape(1, dg), sb_w.reshape(1, dg), w_out.astype(BF16),
      ffn_w.reshape(1, d), w_gate.astype(BF16), w_up.astype(BF16), w_down.astype(BF16))


def kernel(x, attn_norm_w, w_in, q_norm_w, k_norm_w, dil_out_norm_w, sb_out_norm_w, w_out,
           ffn_norm_w, w_gate, w_up, w_down):
    b, s, d = x.shape
    for l in range(attn_norm_w.shape[0]):
        x2 = x.reshape(b * s, d)
        qa, ka, va, qs, ks, vs = _proj_call(x2, attn_norm_w[l], w_in[l], q_norm_w[l], k_norm_w[l], s)
        dg = qa.shape[1]
        qa, ka, va, qs, ks, vs = (a.reshape(b, s, dg) for a in (qa, ka, va, qs, ks, vs))

        o_dil = _dilated_call(qa, ka, va)
        o_sb = _sb_call(qs, ks, vs)

        x = _out_ffn_call(x2, o_dil.reshape(b * s, dg), o_sb.reshape(b * s, dg), dil_out_norm_w[l],
                          sb_out_norm_w[l], w_out[l], ffn_norm_w[l], w_gate[l], w_up[l],
                          w_down[l]).reshape(b, s, d)
    return x
```

```python
import functools

import jax
import jax.numpy as jnp
from jax import lax
from jax.experimental import pallas as pl
from jax.experimental.pallas import tpu as pltpu

F32 = jnp.float32
BF16 = jnp.bfloat16

HEAD_DIM = 64
HALF = HEAD_DIM // 2
LANES = 128
MXU_DIM = 256
N_BACK = 128
BLOCK = 128
DIL_GROUP = 1
SB_Q = 128
SB_K = MXU_DIM
SB_PAD = SB_K - SB_Q
SB_DEAD = 127.0
LOG2E = 1.4426950408889634
ROPE_THETA = 10000.0
EPS = 1e-6
VMEM_LIMIT = 56 * 1024 * 1024


def _dot(a, b):
    return jnp.dot(a, b, preferred_element_type=F32)


def _dot_nt(a, b):
    return lax.dot_general(a, b, (((1,), (1,)), ((), ())), preferred_element_type=F32)


def _rmsnorm(x, w):
    return x * lax.rsqrt(jnp.mean(x * x, axis=-1, keepdims=True) + EPS) * w


def _software_pipeline(n, stages, between=None):
    order = [0] + list(range(len(stages) - 1, 0, -1))
    for t in range(n + len(stages) - 1):
        for k in order:
            if 0 <= t - k < n:
                stages[k](t - k)
        if between is not None:
            between(t)


def _proj_kernel(x_ref, nw_ref, w_ref, g_ref, qw_ref, kw_ref, cos_ref, sin_ref,
                 qa_ref, ka_ref, va_ref, qs_ref, ks_ref, vs_ref):
    dg = qa_ref.shape[-1]
    h = _rmsnorm(x_ref[...], nw_ref[...]).astype(BF16)
    lane = lax.broadcasted_iota(jnp.int32, (1, LANES), 1)
    first_half = (lane % HEAD_DIM) < HALF
    cos = cos_ref[...]
    sin = sin_ref[...]

    def head_norm_rope(y, w_full):
        outs = []
        for j in range(dg // MXU_DIM):
            yc = y[:, j * MXU_DIM:(j + 1) * MXU_DIM]
            ss = _dot((yc * yc).astype(BF16), g_ref[...])
            yn = yc * lax.rsqrt(ss * (1.0 / HEAD_DIM) + EPS) * w_full[:, j * MXU_DIM:(j + 1) * MXU_DIM]
            for i in range(MXU_DIM // LANES):
                v = yn[:, i * LANES:(i + 1) * LANES]
                partner = jnp.where(first_half, pltpu.roll(v, LANES - HALF, 1), pltpu.roll(v, HALF, 1))
                outs.append(v * cos + partner * sin)
        return jnp.concatenate(outs, axis=1)

    def proj(n):
        return _dot(h, w_ref[:, n * dg:(n + 1) * dg])

    scale = HEAD_DIM ** -0.5
    ys = [proj(0)]
    for n, (ref, post) in enumerate((
            (qa_ref, lambda y: head_norm_rope(y, qw_ref[...]) * scale),
            (ka_ref, lambda y: head_norm_rope(y, kw_ref[...])),
            (va_ref, lambda y: y),
            (qs_ref, lambda y: y * scale),
            (ks_ref, lambda y: y),
            (vs_ref, lambda y: y))):
        if n + 1 < 6:
            ys.append(proj(n + 1))
        ref[...] = post(ys[n]).astype(BF16)


def _proj_call(x2, attn_norm_w, w_in, q_norm_w, k_norm_w, seq):
    t, d = x2.shape
    dg = w_in.shape[1] // 6
    tm = 512
    n_heads = dg // HEAD_DIM
    pos = jnp.arange(seq, dtype=F32)
    inv_freq = ROPE_THETA ** (-jnp.arange(0, HEAD_DIM, 2, dtype=F32) / HEAD_DIM)
    ang = pos[:, None] * inv_freq[None, :]
    cos_t = jnp.tile(jnp.cos(ang), (1, LANES // HALF))
    sin_t = jnp.tile(jnp.concatenate([-jnp.sin(ang), jnp.sin(ang)], axis=1), (1, LANES // HEAD_DIM))
    hid = jnp.arange(MXU_DIM) // HEAD_DIM
    g = (hid[:, None] == hid[None, :]).astype(BF16)
    qw = jnp.tile(q_norm_w.reshape(1, HEAD_DIM), (1, n_heads))
    kw = jnp.tile(k_norm_w.reshape(1, HEAD_DIM), (1, n_heads))
    tiles_per_seq = seq // tm
    row = lambda i: (i, 0)
    const = lambda i: (0, 0)
    out = jax.ShapeDtypeStruct((t, dg), BF16)
    return pl.pallas_call(
        _proj_kernel,
        grid=(t // tm,),
        in_specs=[
            pl.BlockSpec((tm, d), row),
            pl.BlockSpec((1, d), const),
            pl.BlockSpec(w_in.shape, const),
            pl.BlockSpec(g.shape, const),
            pl.BlockSpec((1, dg), const),
            pl.BlockSpec((1, dg), const),
            pl.BlockSpec((tm, LANES), lambda i: (i % tiles_per_seq, 0)),
            pl.BlockSpec((tm, LANES), lambda i: (i % tiles_per_seq, 0)),
        ],
        out_specs=[pl.BlockSpec((tm, dg), row)] * 6,
        out_shape=[out] * 6,
        compiler_params=pltpu.CompilerParams(
            dimension_semantics=("arbitrary",), vmem_limit_bytes=VMEM_LIMIT),
        name="proj",
    )(x2, attn_norm_w.reshape(1, d), w_in.astype(BF16), g, qw, kw, cos_t, sin_t)


def _dilated_kernel(q_ref, k_ref, v_ref, o_ref, f_ref, f4_ref, q4_ref, k4_ref, v4_ref,
                    q16_ref, k16_ref, v16_ref, bias_ref, s_ref, p_ref, md_ref, *state_refs):
    sa_refs, sb_refs = state_refs[:3], state_refs[3:]
    s = q_ref.shape[1]
    quarter, sixteenth = s // 4, s // 16
    n_blocks = s // BLOCK

    prep = []
    for n, (src, d4, d16) in enumerate(((q_ref, q4_ref, q16_ref), (k_ref, k4_ref, k16_ref),
                                        (v_ref, v4_ref, v16_ref))):
        def widen(c, n=n, src=src):
            rows = pl.ds(c * quarter, quarter)
            f_ref[n, rows, :] = src[0, rows, :].astype(F32)

        def by4(c, n=n, d4=d4):
            chunk = f_ref[n, pl.ds(c, quarter, stride=4), :]
            f4_ref[n, pl.ds(c * quarter, quarter), :] = chunk
            d4[pl.ds(c * quarter, quarter), :] = chunk.astype(BF16)

        def by16(c, n=n, d16=d16):
            chunk = f4_ref[n, pl.ds((c % 4) * quarter + c // 4, sixteenth, stride=4), :]
            d16[pl.ds(c * sixteenth, sixteenth), :] = chunk.astype(BF16)

        prep += [functools.partial(widen, c) for c in range(4)]
        prep += [functools.partial(by4, c) for c in range(4)]
        prep += [functools.partial(by16, c) for c in range(16)]

    lane = lax.broadcasted_iota(jnp.int32, (1, LANES), 1)
    head0 = lane < HEAD_DIM
    qi = lax.broadcasted_iota(jnp.int32, (2 * BLOCK, 2 * BLOCK), 0) % BLOCK
    kj = lax.broadcasted_iota(jnp.int32, (2 * BLOCK, 2 * BLOCK), 1)
    dist = qi + BLOCK - kj
    bias_ref[...] = jnp.where((dist >= 0) & (dist <= N_BACK), 0.0, -jnp.inf)

    def unstack(a):
        return jnp.where(head0, a[:BLOCK], a[BLOCK:])

    def fold(state, new):
        (acc_p, m_p, den_p), (acc, m, den) = state, new
        mx = jnp.maximum(m_p, m)
        w_p = jnp.exp2(m_p - mx)
        w_n = jnp.exp2(m - mx)
        return acc_p * w_p + acc * w_n, mx, den_p * w_p + den * w_n

    def keep1(rows, new):
        for ref, val in zip(sa_refs, new):
            ref[rows, :] = val

    def fold4(rows, new):
        for ref, val in zip(sb_refs, fold([ref[rows, :] for ref in sb_refs], new)):
            ref[rows, :] = val

    def finish16(rows, new):
        acc, _, den = fold([ref[rows, :] for ref in sa_refs], new)
        sb_refs[0][rows, :] = acc / den

    nat = lambda ref: ref.at[0]
    branches = ((nat(q_ref), nat(k_ref), nat(v_ref), n_blocks, keep1),
                (q4_ref, k4_ref, v4_ref, quarter // BLOCK, fold4),
                (q16_ref, k16_ref, v16_ref, sixteenth // BLOCK, finish16))
    groups_per_branch = n_blocks // DIL_GROUP
    n_groups = len(branches) * groups_per_branch

    def blocks_of(g):
        branch = branches[g // groups_per_branch]
        out = []
        for j in range(DIL_GROUP):
            blk = (g % groups_per_branch) * DIL_GROUP + j
            start = blk * BLOCK
            first = blk % branch[3] == 0
            out.append((branch, start, (start, BLOCK) if first else (start - BLOCK, 2 * BLOCK), first))
        return out

    def stage_scores(g, slot):
        for j, ((q_src, k_src, _, _, _), start, (k0, kn), _) in enumerate(blocks_of(g)):
            q = q_src[pl.ds(start, BLOCK), :]
            zero = jnp.zeros_like(q)
            qs = jnp.concatenate([jnp.where(head0, q, zero), jnp.where(head0, zero, q)], axis=0)
            s_ref[slot, j, :, pl.ds(0, kn)] = _dot_nt(qs, k_src[pl.ds(k0, kn), :])

    def stage_softmax(g, slot):
        for j, (_, _, (_, kn), first) in enumerate(blocks_of(g)):
            bias = bias_ref[:, BLOCK:] if first else bias_ref[...]
            sc = s_ref[slot, j, :, pl.ds(0, kn)] * LOG2E + bias
            m = jnp.max(sc, axis=-1, keepdims=True)
            p = jnp.exp2(sc - m)
            p_ref[slot, j, :, pl.ds(0, kn)] = p.astype(BF16)
            md_ref[slot, j, 0] = unstack(m)
            md_ref[slot, j, 1] = unstack(jnp.sum(p, axis=-1, keepdims=True))

    def stage_values(g, slot):
        if g > 0 and g % groups_per_branch == 0:
            regroup(g // groups_per_branch)
        for j, ((_, _, v_src, _, post), start, (k0, kn), _) in enumerate(blocks_of(g)):
            acc = unstack(_dot(p_ref[slot, j, :, pl.ds(0, kn)], v_src[pl.ds(k0, kn), :]))
            post(pl.ds(start, BLOCK), (acc, md_ref[slot, j, 0], md_ref[slot, j, 1]))

    def regroup(next_branch):
        if next_branch == 1:
            for c in range(4):
                for a_ref, b_ref in zip(sa_refs, sb_refs):
                    b_ref[pl.ds(c * quarter, quarter), :] = a_ref[pl.ds(c, quarter, stride=4), :]
        else:
            for c in range(16):
                src = pl.ds((c % 4) * quarter + c // 4, sixteenth, stride=4)
                for a_ref, b_ref in zip(sa_refs, sb_refs):
                    a_ref[pl.ds(c * sixteenth, sixteenth), :] = b_ref[src, :]

    per_step = -(-len(prep) // (groups_per_branch - 2))

    def between(step):
        for piece in prep[step * per_step:(step + 1) * per_step]:
            piece()

    _software_pipeline(n_groups, [lambda g, stage=stage: stage(g, g % 2)
                                  for stage in (stage_scores, stage_softmax, stage_values)], between)

    o16_ref, o4_ref = sb_refs[0], sa_refs[0]
    for c in range(16):
        o4_ref[pl.ds((c % 4) * quarter + c // 4, sixteenth, stride=4), :] = (
            o16_ref[pl.ds(c * sixteenth, sixteenth), :])
    for c in range(4):
        o_ref[0, pl.ds(c, quarter, stride=4), :] = o4_ref[pl.ds(c * quarter, quarter), :]


def _dilated_call(qa, ka, va):
    b, s, dg = qa.shape
    pairs = dg // LANES
    spec = pl.BlockSpec((1, s, LANES), lambda bi, hp: (bi, 0, hp))
    slab = lambda dt: pltpu.VMEM((s, LANES), dt)
    tile = (2, DIL_GROUP, 2 * BLOCK, 2 * BLOCK)
    return pl.pallas_call(
        _dilated_kernel,
        grid=(b, pairs),
        in_specs=[spec] * 3,
        out_specs=spec,
        out_shape=jax.ShapeDtypeStruct((b, s, dg), F32),
        scratch_shapes=([pltpu.VMEM((3, s, LANES), F32)] * 2 + [slab(BF16)] * 6 + [
                        pltpu.VMEM((2 * BLOCK, 2 * BLOCK), F32),
                        pltpu.VMEM(tile, F32), pltpu.VMEM(tile, BF16),
                        pltpu.VMEM((2, DIL_GROUP, 2, BLOCK, LANES), F32)] + [slab(F32)] * 6),
        compiler_params=pltpu.CompilerParams(
            dimension_semantics=("arbitrary",) * 2, vmem_limit_bytes=VMEM_LIMIT),
        name="dilated",
    )(qa, ka, va)


def _sb_kernel(q_ref, k_ref, v_ref, u_ref, o_ref, kp_ref, vp_ref, z_ref, w_ref, shift_ref, live_ref):
    s = q_ref.shape[1]
    kp_ref[pl.ds(0, SB_PAD), :] = jnp.zeros((SB_PAD, LANES), BF16)
    vp_ref[pl.ds(0, SB_PAD), :] = jnp.zeros((SB_PAD, LANES), BF16)
    kp_ref[pl.ds(SB_PAD, s), :] = k_ref[0]
    vp_ref[pl.ds(SB_PAD, s), :] = v_ref[0]

    lane = lax.broadcasted_iota(jnp.int32, (1, LANES), 1)
    head0 = lane < HEAD_DIM
    ri = lax.broadcasted_iota(jnp.int32, (2 * SB_Q, SB_K), 0) % SB_Q
    cj = lax.broadcasted_iota(jnp.int32, (2 * SB_Q, SB_K), 1)
    causal = cj < ri + (SB_K - SB_Q)
    u = u_ref[...]

    def neg_log_keep(z, mask):
        nk = jnp.maximum(z, 0.0) + jnp.log2(1.0 + jnp.exp2(-jnp.abs(z)))
        return nk if mask is None else jnp.where(mask, nk, 0.0)

    def inclusive_suffix(nk):
        hi = nk.astype(BF16)
        lo = (nk - hi.astype(F32)).astype(BF16)
        cs = _dot(jnp.concatenate([hi, lo], axis=0), u)
        return cs[:2 * SB_Q] + cs[2 * SB_Q:]

    def weights(z, cs, shift, mask):
        x = z - cs
        if shift is not None:
            x = x - shift
        a = jnp.exp2(x)
        if mask is not None:
            a = jnp.where(mask, a, 0.0)
        return a.astype(BF16)

    def live(shift):
        return jnp.min(shift) < SB_DEAD

    def block_start(qb):
        return qb * SB_Q if isinstance(qb, int) else pl.multiple_of(qb * SB_Q, SB_Q)

    def stacked_q(qb):
        q = q_ref[0, pl.ds(block_start(qb), SB_Q), :]
        zero = jnp.zeros_like(q)
        return jnp.concatenate([jnp.where(head0, q, zero), jnp.where(head0, zero, q)], axis=0)

    def tile_keys(qb, i):
        return pl.ds(block_start(qb - 1 + SB_PAD // SB_Q - 2 * i), SB_K)

    def score(qb, i):
        return _dot_nt(stacked_q(qb), kp_ref[tile_keys(qb, i), :])

    def unstack(acc):
        return jnp.where(head0, acc[:SB_Q], acc[SB_Q:])

    def stage_scores(qb):
        z_ref[qb % 2] = score(qb, 0)

    def stage_weights(qb):
        z = z_ref[qb % 2] * LOG2E
        nk = neg_log_keep(z, causal)
        w_ref[qb % 2] = weights(z, inclusive_suffix(nk), None, causal)
        shift = jnp.sum(nk, axis=-1, keepdims=True)
        shift_ref[qb] = shift
        live_ref[qb] = live(shift).astype(jnp.int32)

    def stage_values(qb):
        o_ref[0, pl.ds(qb * SB_Q, SB_Q), :] = unstack(_dot(w_ref[qb % 2], vp_ref[tile_keys(qb, 0), :]))

    _software_pipeline(s // SB_Q, [stage_scores, stage_weights, stage_values])

    def tail(qb):
        def cond(state):
            return (state[0] <= qb // 2) & state[1]

        def body(state):
            i, _, acc, shift = state
            z = score(qb, i) * LOG2E
            nk = neg_log_keep(z, None)
            w = weights(z, inclusive_suffix(nk), shift, None)
            shift = shift + jnp.sum(nk, axis=-1, keepdims=True)
            return i + 1, live(shift), acc + _dot(w, vp_ref[tile_keys(qb, i), :]), shift

        state = (jnp.int32(1), jnp.bool_(True), jnp.zeros((2 * SB_Q, LANES), F32), shift_ref[qb])
        o_ref[0, pl.ds(block_start(qb), SB_Q), :] += unstack(lax.while_loop(cond, body, state)[2])

    def tail_body(qb, carry):
        @pl.when(live_ref[qb] != 0)
        def _():
            tail(qb)
        return carry
    lax.fori_loop(0, s // SB_Q, tail_body, 0)


def _sb_call(qs, ks, vs):
    b, s, dg = qs.shape
    pairs = dg // LANES
    idx = jnp.arange(SB_K)
    u = (idx[:, None] >= idx[None, :]).astype(BF16)
    spec = pl.BlockSpec((1, s, LANES), lambda bi, hp: (bi, 0, hp))
    return pl.pallas_call(
        _sb_kernel,
        grid=(b, pairs),
        in_specs=[spec, spec, spec, pl.BlockSpec((SB_K, SB_K), lambda bi, hp: (0, 0))],
        out_specs=spec,
        out_shape=jax.ShapeDtypeStruct((b, s, dg), F32),
        scratch_shapes=[pltpu.VMEM((s + SB_PAD, LANES), BF16)] * 2 + [
            pltpu.VMEM((2, 2 * SB_Q, SB_K), F32),
            pltpu.VMEM((2, 2 * SB_Q, SB_K), BF16),
            pltpu.VMEM((s // SB_Q, 2 * SB_Q, 1), F32),
            pltpu.SMEM((s // SB_Q,), jnp.int32)],
        compiler_params=pltpu.CompilerParams(
            dimension_semantics=("arbitrary",) * 2, vmem_limit_bytes=VMEM_LIMIT),
        name="stick_breaking",
    )(qs, ks, vs, u)


def _out_ffn_kernel(x_ref, od_ref, os_ref, dw_ref, sw_ref, wo_ref, fw_ref, wg_ref, wu_ref, wd_ref,
                    out_ref, x1_ref, act_ref, *, f_chunk):
    dg = od_ref.shape[-1]
    nd = _rmsnorm(od_ref[...], dw_ref[...]).astype(BF16)
    ns = _rmsnorm(os_ref[...], sw_ref[...]).astype(BF16)
    x1 = x_ref[...] + _dot(nd, wo_ref[:dg, :]) + _dot(ns, wo_ref[dg:, :])
    h = _rmsnorm(x1, fw_ref[...]).astype(BF16)
    x1_ref[...] = x1

    def gate_up(i):
        cols = slice(i * f_chunk, (i + 1) * f_chunk)
        return _dot(h, wg_ref[:, cols]), _dot(h, wu_ref[:, cols])

    n_chunks = wg_ref.shape[1] // f_chunk
    nxt = gate_up(0)
    for i in range(n_chunks):
        g, up = nxt
        if i + 1 < n_chunks:
            nxt = gate_up(i + 1)
        act_ref[:, i * f_chunk:(i + 1) * f_chunk] = (g / (1.0 + jnp.exp(-g)) * up).astype(BF16)
    out_ref[...] = x1_ref[...] + _dot(act_ref[...], wd_ref[...])


def _out_ffn_call(x2, o_dil, o_sb, dil_w, sb_w, w_out, ffn_w, w_gate, w_up, w_down):
    t, d = x2.shape
    dg = o_dil.shape[1]
    d_ff = w_gate.shape[1]
    tm = 512
    f_chunk = 256
    row = lambda i: (i, 0)
    const = lambda i: (0, 0)
    resident = lambda shape: pl.BlockSpec(shape, const, pipeline_mode=pl.Buffered(1))
    return pl.pallas_call(
        functools.partial(_out_ffn_kernel, f_chunk=f_chunk),
        grid=(t // tm,),
        in_specs=[
            pl.BlockSpec((tm, d), row),
            pl.BlockSpec((tm, dg), row),
            pl.BlockSpec((tm, dg), row),
            pl.BlockSpec((1, dg), const),
            pl.BlockSpec((1, dg), const),
            resident((2 * dg, d)),
            pl.BlockSpec((1, d), const),
            resident((d, d_ff)),
            resident((d, d_ff)),
            resident((d_ff, d)),
        ],
        out_specs=pl.BlockSpec((tm, d), row),
        out_shape=jax.ShapeDtypeStruct((t, d), F32),
        scratch_shapes=[pltpu.VMEM((tm, d), F32), pltpu.VMEM((tm, d_ff), BF16)],
        compiler_params=pltpu.CompilerParams(
            dimension_semantics=("arbitrary",), vmem_limit_bytes=VMEM_LIMIT),
        name="out_ffn",
    )(x2, o_dil, o_sb, dil_w.reshape(1, dg), sb_w.reshape(1, dg), w_out.astype(BF16),
      ffn_w.reshape(1, d), w_gate.astype(BF16), w_up.astype(BF16), w_down.astype(BF16))


def kernel(x, attn_norm_w, w_in, q_norm_w, k_norm_w, dil_out_norm_w, sb_out_norm_w, w_out,
           ffn_norm_w, w_gate, w_up, w_down):
    b, s, d = x.shape
    for l in range(attn_norm_w.shape[0]):
        x2 = x.reshape(b * s, d)
        qa, ka, va, qs, ks, vs = _proj_call(x2, attn_norm_w[l], w_in[l], q_norm_w[l], k_norm_w[l], s)
        dg = qa.shape[1]
        qa, ka, va, qs, ks, vs = (a.reshape(b, s, dg) for a in (qa, ka, va, qs, ks, vs))

        o_dil = _dilated_call(qa, ka, va)
        o_sb = _sb_call(qs, ks, vs)

        x = _out_ffn_call(x2, o_dil.reshape(b * s, dg), o_sb.reshape(b * s, dg), dil_out_norm_w[l],
                          sb_out_norm_w[l], w_out[l], ffn_norm_w[l], w_gate[l], w_up[l],
                          w_down[l]).reshape(b, s, d)
    return x
```

```python
import functools

import jax
import jax.numpy as jnp
from jax import lax
from jax.experimental import pallas as pl
from jax.experimental.pallas import tpu as pltpu

F32 = jnp.float32
BF16 = jnp.bfloat16

HEAD_DIM = 64
HALF = HEAD_DIM // 2
LANES = 128
MXU_DIM = 256
PROJ_SUB = 512
N_BACK = 128
BLOCK = 128
DIL_GROUP = 1
SB_Q = 128
SB_K = MXU_DIM
SB_PAD = SB_K - SB_Q
SB_DEAD = 127.0
LOG2E = 1.4426950408889634
ROPE_THETA = 10000.0
EPS = 1e-6
VMEM_LIMIT = 56 * 1024 * 1024


def _dot(a, b):
    return jnp.dot(a, b, preferred_element_type=F32)


def _dot_nt(a, b):
    return lax.dot_general(a, b, (((1,), (1,)), ((), ())), preferred_element_type=F32)


def _rmsnorm(x, w):
    return x * lax.rsqrt(jnp.mean(x * x, axis=-1, keepdims=True) + EPS) * w


def _software_pipeline(n, stages, between=None):
    order = [0] + list(range(len(stages) - 1, 0, -1))
    for t in range(n + len(stages) - 1):
        for k in order:
            if 0 <= t - k < n:
                stages[k](t - k)
        if between is not None:
            between(t)


def _proj_kernel(x_ref, nw_ref, w_ref, g_ref, qw_ref, kw_ref, cos_ref, sin_ref,
                 qa_ref, ka_ref, va_ref, qs_ref, ks_ref, vs_ref):
    dg = qa_ref.shape[-1]
    lane = lax.broadcasted_iota(jnp.int32, (1, LANES), 1)
    first_half = (lane % HEAD_DIM) < HALF
    scale = HEAD_DIM ** -0.5

    def head_norm_rope(y, w_full, rows):
        cos = cos_ref[rows, :]
        sin = sin_ref[rows, :]
        outs = []
        for j in range(dg // MXU_DIM):
            yc = y[:, j * MXU_DIM:(j + 1) * MXU_DIM]
            ss = _dot((yc * yc).astype(BF16), g_ref[...])
            yn = yc * lax.rsqrt(ss * (1.0 / HEAD_DIM) + EPS) * w_full[:, j * MXU_DIM:(j + 1) * MXU_DIM]
            for i in range(MXU_DIM // LANES):
                v = yn[:, i * LANES:(i + 1) * LANES]
                partner = jnp.where(first_half, pltpu.roll(v, LANES - HALF, 1), pltpu.roll(v, HALF, 1))
                outs.append(v * cos + partner * sin)
        return jnp.concatenate(outs, axis=1)

    posts = ((qa_ref, lambda y, rows: head_norm_rope(y, qw_ref[...], rows) * scale),
             (ka_ref, lambda y, rows: head_norm_rope(y, kw_ref[...], rows)),
             (va_ref, lambda y, rows: y),
             (qs_ref, lambda y, rows: y * scale),
             (ks_ref, lambda y, rows: y),
             (vs_ref, lambda y, rows: y))

    n_parts = x_ref.shape[0] // PROJ_SUB
    hs = {}

    def normed(part):
        if part not in hs:
            hs[part] = _rmsnorm(x_ref[pl.ds(part * PROJ_SUB, PROJ_SUB), :], nw_ref[...]).astype(BF16)
        return hs[part]

    units = [(part, n) for part in range(n_parts) for n in range(len(posts))]
    matmul = lambda part, n: _dot(normed(part), w_ref[:, n * dg:(n + 1) * dg])
    ys = [matmul(*units[0])]
    for u, (part, n) in enumerate(units):
        if u + 1 < len(units):
            ys.append(matmul(*units[u + 1]))
        if n == 0 and part + 1 < n_parts:
            normed(part + 1)
        rows = pl.ds(part * PROJ_SUB, PROJ_SUB)
        ref, post = posts[n]
        ref[rows, :] = post(ys[u], rows).astype(BF16)


def _proj_call(x2, attn_norm_w, w_in, q_norm_w, k_norm_w, seq):
    t, d = x2.shape
    dg = w_in.shape[1] // 6
    tm = 2 * PROJ_SUB
    n_heads = dg // HEAD_DIM
    pos = jnp.arange(seq, dtype=F32)
    inv_freq = ROPE_THETA ** (-jnp.arange(0, HEAD_DIM, 2, dtype=F32) / HEAD_DIM)
    ang = pos[:, None] * inv_freq[None, :]
    cos_t = jnp.tile(jnp.cos(ang), (1, LANES // HALF))
    sin_t = jnp.tile(jnp.concatenate([-jnp.sin(ang), jnp.sin(ang)], axis=1), (1, LANES // HEAD_DIM))
    hid = jnp.arange(MXU_DIM) // HEAD_DIM
    g = (hid[:, None] == hid[None, :]).astype(BF16)
    qw = jnp.tile(q_norm_w.reshape(1, HEAD_DIM), (1, n_heads))
    kw = jnp.tile(k_norm_w.reshape(1, HEAD_DIM), (1, n_heads))
    tiles_per_seq = seq // tm
    row = lambda i: (i, 0)
    const = lambda i: (0, 0)
    out = jax.ShapeDtypeStruct((t, dg), BF16)
    return pl.pallas_call(
        _proj_kernel,
        grid=(t // tm,),
        in_specs=[
            pl.BlockSpec((tm, d), row),
            pl.BlockSpec((1, d), const),
            pl.BlockSpec(w_in.shape, const, pipeline_mode=pl.Buffered(1)),
            pl.BlockSpec(g.shape, const),
            pl.BlockSpec((1, dg), const),
            pl.BlockSpec((1, dg), const),
            pl.BlockSpec((tm, LANES), lambda i: (i % tiles_per_seq, 0)),
            pl.BlockSpec((tm, LANES), lambda i: (i % tiles_per_seq, 0)),
        ],
        out_specs=[pl.BlockSpec((tm, dg), row)] * 6,
        out_shape=[out] * 6,
        compiler_params=pltpu.CompilerParams(
            dimension_semantics=("arbitrary",), vmem_limit_bytes=VMEM_LIMIT),
        name="proj",
    )(x2, attn_norm_w.reshape(1, d), w_in.astype(BF16), g, qw, kw, cos_t, sin_t)


def _dilated_kernel(q_ref, k_ref, v_ref, o_ref, f_ref, f4_ref, q4_ref, k4_ref, v4_ref,
                    q16_ref, k16_ref, v16_ref, bias_ref, p_ref, md_ref, *state_refs):
    sa_refs, sb_refs = state_refs[:3], state_refs[3:]
    s = q_ref.shape[1]
    quarter, sixteenth = s // 4, s // 16
    n_blocks = s // BLOCK

    prep = []
    for n, (src, d4, d16) in enumerate(((q_ref, q4_ref, q16_ref), (k_ref, k4_ref, k16_ref),
                                        (v_ref, v4_ref, v16_ref))):
        def widen(c, n=n, src=src):
            rows = pl.ds(c * quarter, quarter)
            f_ref[n, rows, :] = src[0, rows, :].astype(F32)

        def by4(c, n=n, d4=d4):
            chunk = f_ref[n, pl.ds(c, quarter, stride=4), :]
            f4_ref[n, pl.ds(c * quarter, quarter), :] = chunk
            d4[pl.ds(c * quarter, quarter), :] = chunk.astype(BF16)

        def by16(c, n=n, d16=d16):
            chunk = f4_ref[n, pl.ds((c % 4) * quarter + c // 4, sixteenth, stride=4), :]
            d16[pl.ds(c * sixteenth, sixteenth), :] = chunk.astype(BF16)

        prep += [functools.partial(widen, c) for c in range(4)]
        prep += [functools.partial(by4, c) for c in range(4)]
        prep += [functools.partial(by16, c) for c in range(16)]

    lane = lax.broadcasted_iota(jnp.int32, (1, LANES), 1)
    head0 = lane < HEAD_DIM
    qi = lax.broadcasted_iota(jnp.int32, (2 * BLOCK, 2 * BLOCK), 0) % BLOCK
    kj = lax.broadcasted_iota(jnp.int32, (2 * BLOCK, 2 * BLOCK), 1)
    dist = qi + BLOCK - kj
    bias_ref[...] = jnp.where((dist >= 0) & (dist <= N_BACK), 0.0, -jnp.inf)

    def unstack(a):
        return jnp.where(head0, a[:BLOCK], a[BLOCK:])

    def fold(state, new):
        (acc_p, m_p, den_p), (acc, m, den) = state, new
        mx = jnp.maximum(m_p, m)
        w_p = jnp.exp2(m_p - mx)
        w_n = jnp.exp2(m - mx)
        return acc_p * w_p + acc * w_n, mx, den_p * w_p + den * w_n

    def hand_on(src_refs, dst_refs, start, dst_of):
        for src, dst in zip(src_refs, dst_refs):
            for c in range(4):
                dst[pl.ds(dst_of(c), BLOCK // 4), :] = src[pl.ds(start + c, BLOCK // 4, stride=4), :]

    def keep1(start, new):
        for ref, val in zip(sa_refs, new):
            ref[pl.ds(start, BLOCK), :] = val
        hand_on(sa_refs, sb_refs, start, lambda c: c * quarter + start // 4)

    def fold4(start, new):
        rows = pl.ds(start, BLOCK)
        for ref, val in zip(sb_refs, fold([ref[rows, :] for ref in sb_refs], new)):
            ref[rows, :] = val
        c4, within = start // quarter, start % quarter
        hand_on(sb_refs, sa_refs, start, lambda c: (c4 + 4 * c) * sixteenth + within // 4)

    def finish16(start, new):
        acc, _, den = fold([ref[pl.ds(start, BLOCK), :] for ref in sa_refs], new)
        c16, within = start // sixteenth, start % sixteenth
        o_ref[0, pl.ds(16 * within + c16, BLOCK, stride=16), :] = acc / den

    nat = lambda ref: ref.at[0]
    branches = ((nat(q_ref), nat(k_ref), nat(v_ref), n_blocks, keep1),
                (q4_ref, k4_ref, v4_ref, quarter // BLOCK, fold4),
                (q16_ref, k16_ref, v16_ref, sixteenth // BLOCK, finish16))
    groups_per_branch = n_blocks // DIL_GROUP
    n_groups = len(branches) * groups_per_branch

    def blocks_of(g):
        branch = branches[g // groups_per_branch]
        out = []
        for j in range(DIL_GROUP):
            blk = (g % groups_per_branch) * DIL_GROUP + j
            start = blk * BLOCK
            first = blk % branch[3] == 0
            out.append((branch, start, (start, BLOCK) if first else (start - BLOCK, 2 * BLOCK), first))
        return out

    def stage_softmax(g, slot):
        for j, ((q_src, k_src, _, _, _), start, (k0, kn), first) in enumerate(blocks_of(g)):
            q = q_src[pl.ds(start, BLOCK), :]
            zero = jnp.zeros_like(q)
            qs = jnp.concatenate([jnp.where(head0, q, zero), jnp.where(head0, zero, q)], axis=0)
            bias = bias_ref[:, BLOCK:] if first else bias_ref[...]
            sc = _dot_nt(qs, k_src[pl.ds(k0, kn), :]) * LOG2E + bias
            m = jnp.max(sc, axis=-1, keepdims=True)
            p = jnp.exp2(sc - m)
            p_ref[slot, j, :, pl.ds(0, kn)] = p.astype(BF16)
            md_ref[slot, j, 0] = unstack(m)
            md_ref[slot, j, 1] = unstack(jnp.sum(p, axis=-1, keepdims=True))

    def stage_values(g, slot):
        for j, ((_, _, v_src, _, post), start, (k0, kn), _) in enumerate(blocks_of(g)):
            acc = unstack(_dot(p_ref[slot, j, :, pl.ds(0, kn)], v_src[pl.ds(k0, kn), :]))
            post(start, (acc, md_ref[slot, j, 0], md_ref[slot, j, 1]))

    per_step = -(-len(prep) // (groups_per_branch - 2))

    def between(step):
        for piece in prep[step * per_step:(step + 1) * per_step]:
            piece()

    _software_pipeline(n_groups, [lambda g, stage=stage: stage(g, g % 2)
                                  for stage in (stage_softmax, stage_values)], between)


def _dilated_call(qa, ka, va):
    b, s, dg = qa.shape
    pairs = dg // LANES
    spec = pl.BlockSpec((1, s, LANES), lambda bi, hp: (bi, 0, hp))
    slab = lambda dt: pltpu.VMEM((s, LANES), dt)
    return pl.pallas_call(
        _dilated_kernel,
        grid=(b, pairs),
        in_specs=[spec] * 3,
        out_specs=spec,
        out_shape=jax.ShapeDtypeStruct((b, s, dg), F32),
        scratch_shapes=([pltpu.VMEM((3, s, LANES), F32)] * 2 + [slab(BF16)] * 6 + [
                        pltpu.VMEM((2 * BLOCK, 2 * BLOCK), F32),
                        pltpu.VMEM((2, DIL_GROUP, 2 * BLOCK, 2 * BLOCK), BF16),
                        pltpu.VMEM((2, DIL_GROUP, 2, BLOCK, LANES), F32)] + [slab(F32)] * 6),
        compiler_params=pltpu.CompilerParams(
            dimension_semantics=("arbitrary",) * 2, vmem_limit_bytes=VMEM_LIMIT),
        name="dilated",
    )(qa, ka, va)


def _sb_kernel(q_ref, k_ref, v_ref, u_ref, o_ref, kp_ref, vp_ref, z_ref, w_ref, shift_ref, live_ref):
    s = q_ref.shape[1]
    kp_ref[pl.ds(0, SB_PAD), :] = jnp.zeros((SB_PAD, LANES), BF16)
    vp_ref[pl.ds(0, SB_PAD), :] = jnp.zeros((SB_PAD, LANES), BF16)
    kp_ref[pl.ds(SB_PAD, s), :] = k_ref[0]
    vp_ref[pl.ds(SB_PAD, s), :] = v_ref[0]

    lane = lax.broadcasted_iota(jnp.int32, (1, LANES), 1)
    head0 = lane < HEAD_DIM
    ri = lax.broadcasted_iota(jnp.int32, (2 * SB_Q, SB_K), 0) % SB_Q
    cj = lax.broadcasted_iota(jnp.int32, (2 * SB_Q, SB_K), 1)
    causal = cj < ri + (SB_K - SB_Q)
    u = u_ref[...]

    def neg_log_keep(z, mask):
        nk = jnp.maximum(z, 0.0) + jnp.log2(1.0 + jnp.exp2(-jnp.abs(z)))
        return nk if mask is None else jnp.where(mask, nk, 0.0)

    def inclusive_suffix(nk):
        hi = nk.astype(BF16)
        lo = (nk - hi.astype(F32)).astype(BF16)
        cs = _dot(jnp.concatenate([hi, lo], axis=0), u)
        return cs[:2 * SB_Q] + cs[2 * SB_Q:]

    def weights(z, cs, shift, mask):
        x = z - cs
        if shift is not None:
            x = x - shift
        a = jnp.exp2(x)
        if mask is not None:
            a = jnp.where(mask, a, 0.0)
        return a.astype(BF16)

    def live(shift):
        return jnp.min(shift) < SB_DEAD

    def block_start(qb):
        return qb * SB_Q if isinstance(qb, int) else pl.multiple_of(qb * SB_Q, SB_Q)

    def stacked_q(qb):
        q = q_ref[0, pl.ds(block_start(qb), SB_Q), :]
        zero = jnp.zeros_like(q)
        return jnp.concatenate([jnp.where(head0, q, zero), jnp.where(head0, zero, q)], axis=0)

    def tile_keys(qb, i):
        return pl.ds(block_start(qb - 1 + SB_PAD // SB_Q - 2 * i), SB_K)

    def score(qb, i):
        return _dot_nt(stacked_q(qb), kp_ref[tile_keys(qb, i), :])

    def unstack(acc):
        return jnp.where(head0, acc[:SB_Q], acc[SB_Q:])

    def stage_scores(qb):
        z_ref[qb % 2] = score(qb, 0)

    def stage_weights(qb):
        z = z_ref[qb % 2] * LOG2E
        nk = neg_log_keep(z, causal)
        w_ref[qb % 2] = weights(z, inclusive_suffix(nk), None, causal)
        shift = jnp.sum(nk, axis=-1, keepdims=True)
        shift_ref[qb] = shift
        live_ref[qb] = live(shift).astype(jnp.int32)

    def stage_values(qb):
        o_ref[0, pl.ds(qb * SB_Q, SB_Q), :] = unstack(_dot(w_ref[qb % 2], vp_ref[tile_keys(qb, 0), :]))

    _software_pipeline(s // SB_Q, [stage_scores, stage_weights, stage_values])

    def tail(qb):
        def cond(state):
            return (state[0] <= qb // 2) & state[1]

        def body(state):
            i, _, acc, shift = state
            z = score(qb, i) * LOG2E
            nk = neg_log_keep(z, None)
            w = weights(z, inclusive_suffix(nk), shift, None)
            shift = shift + jnp.sum(nk, axis=-1, keepdims=True)
            return i + 1, live(shift), acc + _dot(w, vp_ref[tile_keys(qb, i), :]), shift

        state = (jnp.int32(1), jnp.bool_(True), jnp.zeros((2 * SB_Q, LANES), F32), shift_ref[qb])
        o_ref[0, pl.ds(block_start(qb), SB_Q), :] += unstack(lax.while_loop(cond, body, state)[2])

    def tail_body(qb, carry):
        @pl.when(live_ref[qb] != 0)
        def _():
            tail(qb)
        return carry
    lax.fori_loop(0, s // SB_Q, tail_body, 0)


def _sb_call(qs, ks, vs):
    b, s, dg = qs.shape
    pairs = dg // LANES
    idx = jnp.arange(SB_K)
    u = (idx[:, None] >= idx[None, :]).astype(BF16)
    spec = pl.BlockSpec((1, s, LANES), lambda bi, hp: (bi, 0, hp))
    return pl.pallas_call(
        _sb_kernel,
        grid=(b, pairs),
        in_specs=[spec, spec, spec, pl.BlockSpec((SB_K, SB_K), lambda bi, hp: (0, 0))],
        out_specs=spec,
        out_shape=jax.ShapeDtypeStruct((b, s, dg), F32),
        scratch_shapes=[pltpu.VMEM((s + SB_PAD, LANES), BF16)] * 2 + [
            pltpu.VMEM((2, 2 * SB_Q, SB_K), F32),
            pltpu.VMEM((2, 2 * SB_Q, SB_K), BF16),
            pltpu.VMEM((s // SB_Q, 2 * SB_Q, 1), F32),
            pltpu.SMEM((s // SB_Q,), jnp.int32)],
        compiler_params=pltpu.CompilerParams(
            dimension_semantics=("arbitrary",) * 2, vmem_limit_bytes=VMEM_LIMIT),
        name="stick_breaking",
    )(qs, ks, vs, u)


def _out_ffn_kernel(x_ref, od_ref, os_ref, dw_ref, sw_ref, wo_ref, fw_ref, wg_ref, wu_ref, wd_ref,
                    out_ref, x1_ref, act_ref, *, f_chunk):
    dg = od_ref.shape[-1]
    nd = _rmsnorm(od_ref[...], dw_ref[...]).astype(BF16)
    ns = _rmsnorm(os_ref[...], sw_ref[...]).astype(BF16)
    x1 = x_ref[...] + _dot(nd, wo_ref[:dg, :]) + _dot(ns, wo_ref[dg:, :])
    h = _rmsnorm(x1, fw_ref[...]).astype(BF16)
    x1_ref[...] = x1

    def gate_up(i):
        cols = slice(i * f_chunk, (i + 1) * f_chunk)
        return _dot(h, wg_ref[:, cols]), _dot(h, wu_ref[:, cols])

    n_chunks = wg_ref.shape[1] // f_chunk
    nxt = gate_up(0)
    for i in range(n_chunks):
        g, up = nxt
        if i + 1 < n_chunks:
            nxt = gate_up(i + 1)
        act_ref[:, i * f_chunk:(i + 1) * f_chunk] = (g / (1.0 + jnp.exp(-g)) * up).astype(BF16)
    out_ref[...] = x1_ref[...] + _dot(act_ref[...], wd_ref[...])


def _out_ffn_call(x2, o_dil, o_sb, dil_w, sb_w, w_out, ffn_w, w_gate, w_up, w_down):
    t, d = x2.shape
    dg = o_dil.shape[1]
    d_ff = w_gate.shape[1]
    tm = 512
    f_chunk = 256
    row = lambda i: (i, 0)
    const = lambda i: (0, 0)
    resident = lambda shape: pl.BlockSpec(shape, const, pipeline_mode=pl.Buffered(1))
    return pl.pallas_call(
        functools.partial(_out_ffn_kernel, f_chunk=f_chunk),
        grid=(t // tm,),
        in_specs=[
            pl.BlockSpec((tm, d), row),
            pl.BlockSpec((tm, dg), row),
            pl.BlockSpec((tm, dg), row),
            pl.BlockSpec((1, dg), const),
            pl.BlockSpec((1, dg), const),
            resident((2 * dg, d)),
            pl.BlockSpec((1, d), const),
            resident((d, d_ff)),
            resident((d, d_ff)),
            resident((d_ff, d)),
        ],
        out_specs=pl.BlockSpec((tm, d), row),
        out_shape=jax.ShapeDtypeStruct((t, d), F32),
        scratch_shapes=[pltpu.VMEM((tm, d), F32), pltpu.VMEM((tm, d_ff), BF16)],
        compiler_params=pltpu.CompilerParams(
            dimension_semantics=("arbitrary",), vmem_limit_bytes=VMEM_LIMIT),
        name="out_ffn",
    )(x2, o_dil, o_sb, dil_w.reshape(1, dg), sb_w.reshape(1, dg), w_out.astype(BF16),
      ffn_w.reshape(1, d), w_gate.astype(BF16), w_up.astype(BF16), w_down.astype(BF16))


def kernel(x, attn_norm_w, w_in, q_norm_w, k_norm_w, dil_out_norm_w, sb_out_norm_w, w_out,
           ffn_norm_w, w_gate, w_up, w_down):
    b, s, d = x.shape
    for l in range(attn_norm_w.shape[0]):
        x2 = x.reshape(b * s, d)
        qa, ka, va, qs, ks, vs = _proj_call(x2, attn_norm_w[l], w_in[l], q_norm_w[l], k_norm_w[l], s)
        dg = qa.shape[1]
        qa, ka, va, qs, ks, vs = (a.reshape(b, s, dg) for a in (qa, ka, va, qs, ks, vs))

        o_dil = _dilated_call(qa, ka, va)
        o_sb = _sb_call(qs, ks, vs)

        x = _out_ffn_call(x2, o_dil.reshape(b * s, dg), o_sb.reshape(b * s, dg), dil_out_norm_w[l],
                          sb_out_norm_w[l], w_out[l], ffn_norm_w[l], w_gate[l], w_up[l],
                          w_down[l]).reshape(b, s, d)
    return x
```

```python
import functools

import jax
import jax.numpy as jnp
from jax import lax
from jax.experimental import pallas as pl
from jax.experimental.pallas import tpu as pltpu

F32 = jnp.float32
BF16 = jnp.bfloat16

HEAD_DIM = 64
HALF = HEAD_DIM // 2
LANES = 128
MXU_DIM = 256
PROJ_SUB = 512
N_BACK = 128
BLOCK = 128
DIL_GROUP = 1
SB_Q = 128
SB_K = MXU_DIM
SB_PAD = SB_K - SB_Q
SB_DEAD = 127.0
LOG2E = 1.4426950408889634
ROPE_THETA = 10000.0
EPS = 1e-6
VMEM_LIMIT = 56 * 1024 * 1024


def _dot(a, b):
    return jnp.dot(a, b, preferred_element_type=F32)


def _dot_nt(a, b):
    return lax.dot_general(a, b, (((1,), (1,)), ((), ())), preferred_element_type=F32)


def _rmsnorm(x, w):
    return x * lax.rsqrt(jnp.mean(x * x, axis=-1, keepdims=True) + EPS) * w


def _software_pipeline(n, stages, between=None):
    order = [0] + list(range(len(stages) - 1, 0, -1))
    for t in range(n + len(stages) - 1):
        for k in order:
            if 0 <= t - k < n:
                stages[k](t - k)
        if between is not None:
            between(t)


def _proj_kernel(x_ref, nw_ref, w_ref, g_ref, qw_ref, kw_ref, cos_ref, sin_ref,
                 qa_ref, ka_ref, va_ref, qs_ref, ks_ref, vs_ref):
    dg = qa_ref.shape[-1]
    lane = lax.broadcasted_iota(jnp.int32, (1, LANES), 1)
    first_half = (lane % HEAD_DIM) < HALF
    scale = HEAD_DIM ** -0.5

    def head_norm_rope(y, w_full, rows):
        cos = cos_ref[rows, :]
        sin = sin_ref[rows, :]
        outs = []
        for j in range(dg // MXU_DIM):
            yc = y[:, j * MXU_DIM:(j + 1) * MXU_DIM]
            ss = _dot((yc * yc).astype(BF16), g_ref[...])
            yn = yc * lax.rsqrt(ss * (1.0 / HEAD_DIM) + EPS) * w_full[:, j * MXU_DIM:(j + 1) * MXU_DIM]
            for i in range(MXU_DIM // LANES):
                v = yn[:, i * LANES:(i + 1) * LANES]
                partner = jnp.where(first_half, pltpu.roll(v, LANES - HALF, 1), pltpu.roll(v, HALF, 1))
                outs.append(v * cos + partner * sin)
        return jnp.concatenate(outs, axis=1)

    posts = ((qa_ref, lambda y, rows: head_norm_rope(y, qw_ref[...], rows) * scale),
             (ka_ref, lambda y, rows: head_norm_rope(y, kw_ref[...], rows)),
             (va_ref, lambda y, rows: y),
             (qs_ref, lambda y, rows: y * scale),
             (ks_ref, lambda y, rows: y),
             (vs_ref, lambda y, rows: y))

    n_parts = x_ref.shape[0] // PROJ_SUB
    hs = {}

    def normed(part):
        if part not in hs:
            hs[part] = _rmsnorm(x_ref[pl.ds(part * PROJ_SUB, PROJ_SUB), :], nw_ref[...]).astype(BF16)
        return hs[part]

    units = [(part, n) for part in range(n_parts) for n in range(len(posts))]
    matmul = lambda part, n: _dot(normed(part), w_ref[:, n * dg:(n + 1) * dg])
    ys = [matmul(*units[0])]
    for u, (part, n) in enumerate(units):
        if u + 1 < len(units):
            ys.append(matmul(*units[u + 1]))
        if n == 0 and part + 1 < n_parts:
            normed(part + 1)
        rows = pl.ds(part * PROJ_SUB, PROJ_SUB)
        ref, post = posts[n]
        ref[rows, :] = post(ys[u], rows).astype(BF16)


def _proj_call(x2, attn_norm_w, w_in, q_norm_w, k_norm_w, seq):
    t, d = x2.shape
    dg = w_in.shape[1] // 6
    tm = 2 * PROJ_SUB
    n_heads = dg // HEAD_DIM
    pos = jnp.arange(seq, dtype=F32)
    inv_freq = ROPE_THETA ** (-jnp.arange(0, HEAD_DIM, 2, dtype=F32) / HEAD_DIM)
    ang = pos[:, None] * inv_freq[None, :]
    cos_t = jnp.tile(jnp.cos(ang), (1, LANES // HALF))
    sin_t = jnp.tile(jnp.concatenate([-jnp.sin(ang), jnp.sin(ang)], axis=1), (1, LANES // HEAD_DIM))
    hid = jnp.arange(MXU_DIM) // HEAD_DIM
    g = (hid[:, None] == hid[None, :]).astype(BF16)
    qw = jnp.tile(q_norm_w.reshape(1, HEAD_DIM), (1, n_heads))
    kw = jnp.tile(k_norm_w.reshape(1, HEAD_DIM), (1, n_heads))
    tiles_per_seq = seq // tm
    row = lambda i: (i, 0)
    const = lambda i: (0, 0)
    out = jax.ShapeDtypeStruct((t, dg), BF16)
    return pl.pallas_call(
        _proj_kernel,
        grid=(t // tm,),
        in_specs=[
            pl.BlockSpec((tm, d), row),
            pl.BlockSpec((1, d), const),
            pl.BlockSpec(w_in.shape, const, pipeline_mode=pl.Buffered(1)),
            pl.BlockSpec(g.shape, const),
            pl.BlockSpec((1, dg), const),
            pl.BlockSpec((1, dg), const),
            pl.BlockSpec((tm, LANES), lambda i: (i % tiles_per_seq, 0)),
            pl.BlockSpec((tm, LANES), lambda i: (i % tiles_per_seq, 0)),
        ],
        out_specs=[pl.BlockSpec((tm, dg), row)] * 6,
        out_shape=[out] * 6,
        compiler_params=pltpu.CompilerParams(
            dimension_semantics=("arbitrary",), vmem_limit_bytes=VMEM_LIMIT),
        name="proj",
    )(x2, attn_norm_w.reshape(1, d), w_in.astype(BF16), g, qw, kw, cos_t, sin_t)


def _dilated_kernel(q_ref, k_ref, v_ref, o_ref, f_ref, f4_ref, q4_ref, k4_ref, v4_ref,
                    q16_ref, k16_ref, v16_ref, bias_ref, p_ref, md_ref, *state_refs):
    sa_refs, sb_refs = state_refs[:3], state_refs[3:]
    s = q_ref.shape[1]
    quarter, sixteenth = s // 4, s // 16
    n_blocks = s // BLOCK

    prep = []
    for n, (src, d4, d16) in enumerate(((q_ref, q4_ref, q16_ref), (k_ref, k4_ref, k16_ref),
                                        (v_ref, v4_ref, v16_ref))):
        def widen(c, n=n, src=src):
            rows = pl.ds(c * quarter, quarter)
            f_ref[n, rows, :] = src[0, rows, :].astype(F32)

        def by4(c, n=n, d4=d4):
            chunk = f_ref[n, pl.ds(c, quarter, stride=4), :]
            f4_ref[n, pl.ds(c * quarter, quarter), :] = chunk
            d4[pl.ds(c * quarter, quarter), :] = chunk.astype(BF16)

        def by16(c, n=n, d16=d16):
            chunk = f4_ref[n, pl.ds((c % 4) * quarter + c // 4, sixteenth, stride=4), :]
            d16[pl.ds(c * sixteenth, sixteenth), :] = chunk.astype(BF16)

        prep += [functools.partial(widen, c) for c in range(4)]
        prep += [functools.partial(by4, c) for c in range(4)]
        prep += [functools.partial(by16, c) for c in range(16)]

    lane = lax.broadcasted_iota(jnp.int32, (1, LANES), 1)
    head0 = lane < HEAD_DIM
    qi = lax.broadcasted_iota(jnp.int32, (2 * BLOCK, 2 * BLOCK), 0) % BLOCK
    kj = lax.broadcasted_iota(jnp.int32, (2 * BLOCK, 2 * BLOCK), 1)
    dist = qi + BLOCK - kj
    bias_ref[...] = jnp.where((dist >= 0) & (dist <= N_BACK), 0.0, -jnp.inf)

    def unstack(a):
        return jnp.where(head0, a[:BLOCK], a[BLOCK:])

    def fold(state, new):
        (acc_p, m_p, den_p), (acc, m, den) = state, new
        mx = jnp.maximum(m_p, m)
        w_p = jnp.exp2(m_p - mx)
        w_n = jnp.exp2(m - mx)
        return acc_p * w_p + acc * w_n, mx, den_p * w_p + den * w_n

    def keep1(start, new):
        for ref, val in zip(sa_refs, new):
            ref[pl.ds(start, BLOCK), :] = val

    def fold4(start, new):
        if start == 0:
            for c in range(4):
                for a_ref, b_ref in zip(sa_refs, sb_refs):
                    b_ref[pl.ds(c * quarter, quarter), :] = a_ref[pl.ds(c, quarter, stride=4), :]
        rows = pl.ds(start, BLOCK)
        for ref, val in zip(sb_refs, fold([ref[rows, :] for ref in sb_refs], new)):
            ref[rows, :] = val

    def finish16(start, new):
        if start == 0:
            for c in range(16):
                src = pl.ds((c % 4) * quarter + c // 4, sixteenth, stride=4)
                for a_ref, b_ref in zip(sa_refs, sb_refs):
                    a_ref[pl.ds(c * sixteenth, sixteenth), :] = b_ref[src, :]
        acc, _, den = fold([ref[pl.ds(start, BLOCK), :] for ref in sa_refs], new)
        sb_refs[0][pl.ds(start, BLOCK), :] = acc / den

    nat = lambda ref: ref.at[0]
    branches = ((nat(q_ref), nat(k_ref), nat(v_ref), n_blocks, keep1),
                (q4_ref, k4_ref, v4_ref, quarter // BLOCK, fold4),
                (q16_ref, k16_ref, v16_ref, sixteenth // BLOCK, finish16))
    groups_per_branch = n_blocks // DIL_GROUP
    n_groups = len(branches) * groups_per_branch

    def blocks_of(g):
        branch = branches[g // groups_per_branch]
        out = []
        for j in range(DIL_GROUP):
            blk = (g % groups_per_branch) * DIL_GROUP + j
            start = blk * BLOCK
            first = blk % branch[3] == 0
            out.append((branch, start, (start, BLOCK) if first else (start - BLOCK, 2 * BLOCK), first))
        return out

    def stage_softmax(g, slot):
        for j, ((q_src, k_src, _, _, _), start, (k0, kn), first) in enumerate(blocks_of(g)):
            q = q_src[pl.ds(start, BLOCK), :]
            zero = jnp.zeros_like(q)
            qs = jnp.concatenate([jnp.where(head0, q, zero), jnp.where(head0, zero, q)], axis=0)
            bias = bias_ref[:, BLOCK:] if first else bias_ref[...]
            sc = _dot_nt(qs, k_src[pl.ds(k0, kn), :]) * LOG2E + bias
            m = jnp.max(sc, axis=-1, keepdims=True)
            p = jnp.exp2(sc - m)
            p_ref[slot, j, :, pl.ds(0, kn)] = p.astype(BF16)
            md_ref[slot, j, 0] = unstack(m)
            md_ref[slot, j, 1] = unstack(jnp.sum(p, axis=-1, keepdims=True))

    def stage_values(g, slot):
        for j, ((_, _, v_src, _, post), start, (k0, kn), _) in enumerate(blocks_of(g)):
            acc = unstack(_dot(p_ref[slot, j, :, pl.ds(0, kn)], v_src[pl.ds(k0, kn), :]))
            post(start, (acc, md_ref[slot, j, 0], md_ref[slot, j, 1]))

    per_step = -(-len(prep) // (groups_per_branch - 2))

    def between(step):
        for piece in prep[step * per_step:(step + 1) * per_step]:
            piece()

    _software_pipeline(n_groups, [lambda g, stage=stage: stage(g, g % 2)
                                  for stage in (stage_softmax, stage_values)], between)

    o16_ref, o4_ref = sb_refs[0], sa_refs[0]
    for c in range(16):
        o4_ref[pl.ds((c % 4) * quarter + c // 4, sixteenth, stride=4), :] = (
            o16_ref[pl.ds(c * sixteenth, sixteenth), :])
    for c in range(4):
        o_ref[0, pl.ds(c, quarter, stride=4), :] = o4_ref[pl.ds(c * quarter, quarter), :]


def _dilated_call(qa, ka, va):
    b, s, dg = qa.shape
    pairs = dg // LANES
    spec = pl.BlockSpec((1, s, LANES), lambda bi, hp: (bi, 0, hp))
    slab = lambda dt: pltpu.VMEM((s, LANES), dt)
    return pl.pallas_call(
        _dilated_kernel,
        grid=(b, pairs),
        in_specs=[spec] * 3,
        out_specs=spec,
        out_shape=jax.ShapeDtypeStruct((b, s, dg), F32),
        scratch_shapes=([pltpu.VMEM((3, s, LANES), F32)] * 2 + [slab(BF16)] * 6 + [
                        pltpu.VMEM((2 * BLOCK, 2 * BLOCK), F32),
                        pltpu.VMEM((2, DIL_GROUP, 2 * BLOCK, 2 * BLOCK), BF16),
                        pltpu.VMEM((2, DIL_GROUP, 2, BLOCK, LANES), F32)] + [slab(F32)] * 6),
        compiler_params=pltpu.CompilerParams(
            dimension_semantics=("arbitrary",) * 2, vmem_limit_bytes=VMEM_LIMIT),
        name="dilated",
    )(qa, ka, va)


def _sb_kernel(q_ref, k_ref, v_ref, u_ref, o_ref, kp_ref, vp_ref, z_ref, w_ref, shift_ref, live_ref):
    s = q_ref.shape[1]
    kp_ref[pl.ds(0, SB_PAD), :] = jnp.zeros((SB_PAD, LANES), BF16)
    vp_ref[pl.ds(0, SB_PAD), :] = jnp.zeros((SB_PAD, LANES), BF16)
    kp_ref[pl.ds(SB_PAD, s), :] = k_ref[0]
    vp_ref[pl.ds(SB_PAD, s), :] = v_ref[0]

    lane = lax.broadcasted_iota(jnp.int32, (1, LANES), 1)
    head0 = lane < HEAD_DIM
    ri = lax.broadcasted_iota(jnp.int32, (2 * SB_Q, SB_K), 0) % SB_Q
    cj = lax.broadcasted_iota(jnp.int32, (2 * SB_Q, SB_K), 1)
    causal = cj < ri + (SB_K - SB_Q)
    u = u_ref[...]

    def neg_log_keep(z, mask):
        nk = jnp.maximum(z, 0.0) + jnp.log2(1.0 + jnp.exp2(-jnp.abs(z)))
        return nk if mask is None else jnp.where(mask, nk, 0.0)

    def inclusive_suffix(nk):
        hi = nk.astype(BF16)
        lo = (nk - hi.astype(F32)).astype(BF16)
        cs = _dot(jnp.concatenate([hi, lo], axis=0), u)
        return cs[:2 * SB_Q] + cs[2 * SB_Q:]

    def weights(z, cs, shift, mask):
        x = z - cs
        if shift is not None:
            x = x - shift
        a = jnp.exp2(x)
        if mask is not None:
            a = jnp.where(mask, a, 0.0)
        return a.astype(BF16)

    def live(shift):
        return jnp.min(shift) < SB_DEAD

    def block_start(qb):
        return qb * SB_Q if isinstance(qb, int) else pl.multiple_of(qb * SB_Q, SB_Q)

    def stacked_q(qb):
        q = q_ref[0, pl.ds(block_start(qb), SB_Q), :]
        zero = jnp.zeros_like(q)
        return jnp.concatenate([jnp.where(head0, q, zero), jnp.where(head0, zero, q)], axis=0)

    def tile_keys(qb, i):
        return pl.ds(block_start(qb - 1 + SB_PAD // SB_Q - 2 * i), SB_K)

    def score(qb, i):
        return _dot_nt(stacked_q(qb), kp_ref[tile_keys(qb, i), :])

    def unstack(acc):
        return jnp.where(head0, acc[:SB_Q], acc[SB_Q:])

    def stage_scores(qb):
        z_ref[qb % 2] = score(qb, 0)

    def stage_weights(qb):
        z = z_ref[qb % 2] * LOG2E
        nk = neg_log_keep(z, causal)
        w_ref[qb % 2] = weights(z, inclusive_suffix(nk), None, causal)
        shift = jnp.sum(nk, axis=-1, keepdims=True)
        shift_ref[qb] = shift
        live_ref[qb] = live(shift).astype(jnp.int32)

    def stage_values(qb):
        o_ref[0, pl.ds(qb * SB_Q, SB_Q), :] = unstack(_dot(w_ref[qb % 2], vp_ref[tile_keys(qb, 0), :]))

    _software_pipeline(s // SB_Q, [stage_scores, stage_weights, stage_values])

    def tail(qb):
        def cond(state):
            return (state[0] <= qb // 2) & state[1]

        def body(state):
            i, _, acc, shift = state
            z = score(qb, i) * LOG2E
            nk = neg_log_keep(z, None)
            w = weights(z, inclusive_suffix(nk), shift, None)
            shift = shift + jnp.sum(nk, axis=-1, keepdims=True)
            return i + 1, live(shift), acc + _dot(w, vp_ref[tile_keys(qb, i), :]), shift

        state = (jnp.int32(1), jnp.bool_(True), jnp.zeros((2 * SB_Q, LANES), F32), shift_ref[qb])
        o_ref[0, pl.ds(block_start(qb), SB_Q), :] += unstack(lax.while_loop(cond, body, state)[2])

    def tail_body(qb, carry):
        @pl.when(live_ref[qb] != 0)
        def _():
            tail(qb)
        return carry
    lax.fori_loop(0, s // SB_Q, tail_body, 0)


def _sb_call(qs, ks, vs):
    b, s, dg = qs.shape
    pairs = dg // LANES
    idx = jnp.arange(SB_K)
    u = (idx[:, None] >= idx[None, :]).astype(BF16)
    spec = pl.BlockSpec((1, s, LANES), lambda bi, hp: (bi, 0, hp))
    return pl.pallas_call(
        _sb_kernel,
        grid=(b, pairs),
        in_specs=[spec, spec, spec, pl.BlockSpec((SB_K, SB_K), lambda bi, hp: (0, 0))],
        out_specs=spec,
        out_shape=jax.ShapeDtypeStruct((b, s, dg), F32),
        scratch_shapes=[pltpu.VMEM((s + SB_PAD, LANES), BF16)] * 2 + [
            pltpu.VMEM((2, 2 * SB_Q, SB_K), F32),
            pltpu.VMEM((2, 2 * SB_Q, SB_K), BF16),
            pltpu.VMEM((s // SB_Q, 2 * SB_Q, 1), F32),
            pltpu.SMEM((s // SB_Q,), jnp.int32)],
        compiler_params=pltpu.CompilerParams(
            dimension_semantics=("arbitrary",) * 2, vmem_limit_bytes=VMEM_LIMIT),
        name="stick_breaking",
    )(qs, ks, vs, u)


def _out_ffn_kernel(x_ref, od_ref, os_ref, dw_ref, sw_ref, wo_ref, fw_ref, wg_ref, wu_ref, wd_ref,
                    out_ref, x1_ref, act_ref, *, f_chunk):
    dg = od_ref.shape[-1]
    nd = _rmsnorm(od_ref[...], dw_ref[...]).astype(BF16)
    ns = _rmsnorm(os_ref[...], sw_ref[...]).astype(BF16)
    x1 = x_ref[...] + _dot(nd, wo_ref[:dg, :]) + _dot(ns, wo_ref[dg:, :])
    h = _rmsnorm(x1, fw_ref[...]).astype(BF16)
    x1_ref[...] = x1

    def gate_up(i):
        cols = slice(i * f_chunk, (i + 1) * f_chunk)
        return _dot(h, wg_ref[:, cols]), _dot(h, wu_ref[:, cols])

    n_chunks = wg_ref.shape[1] // f_chunk
    nxt = gate_up(0)
    for i in range(n_chunks):
        g, up = nxt
        if i + 1 < n_chunks:
            nxt = gate_up(i + 1)
        act_ref[:, i * f_chunk:(i + 1) * f_chunk] = (g / (1.0 + jnp.exp(-g)) * up).astype(BF16)
    out_ref[...] = x1_ref[...] + _dot(act_ref[...], wd_ref[...])


def _out_ffn_call(x2, o_dil, o_sb, dil_w, sb_w, w_out, ffn_w, w_gate, w_up, w_down):
    t, d = x2.shape
    dg = o_dil.shape[1]
    d_ff = w_gate.shape[1]
    tm = 512
    f_chunk = 256
    row = lambda i: (i, 0)
    const = lambda i: (0, 0)
    resident = lambda shape: pl.BlockSpec(shape, const, pipeline_mode=pl.Buffered(1))
    return pl.pallas_call(
        functools.partial(_out_ffn_kernel, f_chunk=f_chunk),
        grid=(t // tm,),
        in_specs=[
            pl.BlockSpec((tm, d), row),
            pl.BlockSpec((tm, dg), row),
            pl.BlockSpec((tm, dg), row),
            pl.BlockSpec((1, dg), const),
            pl.BlockSpec((1, dg), const),
            resident((2 * dg, d)),
            pl.BlockSpec((1, d), const),
            resident((d, d_ff)),
            resident((d, d_ff)),
            resident((d_ff, d)),
        ],
        out_specs=pl.BlockSpec((tm, d), row),
        out_shape=jax.ShapeDtypeStruct((t, d), F32),
        scratch_shapes=[pltpu.VMEM((tm, d), F32), pltpu.VMEM((tm, d_ff), BF16)],
        compiler_params=pltpu.CompilerParams(
            dimension_semantics=("arbitrary",), vmem_limit_bytes=VMEM_LIMIT),
        name="out_ffn",
    )(x2, o_dil, o_sb, dil_w.reshape(1, dg), sb_w.reshape(1, dg), w_out.astype(BF16),
      ffn_w.reshape(1, d), w_gate.astype(BF16), w_up.astype(BF16), w_down.astype(BF16))


def kernel(x, attn_norm_w, w_in, q_norm_w, k_norm_w, dil_out_norm_w, sb_out_norm_w, w_out,
           ffn_norm_w, w_gate, w_up, w_down):
    b, s, d = x.shape
    for l in range(attn_norm_w.shape[0]):
        x2 = x.reshape(b * s, d)
        qa, ka, va, qs, ks, vs = _proj_call(x2, attn_norm_w[l], w_in[l], q_norm_w[l], k_norm_w[l], s)
        dg = qa.shape[1]
        qa, ka, va, qs, ks, vs = (a.reshape(b, s, dg) for a in (qa, ka, va, qs, ks, vs))

        o_dil = _dilated_call(qa, ka, va)
        o_sb = _sb_call(qs, ks, vs)

        x = _out_ffn_call(x2, o_dil.reshape(b * s, dg), o_sb.reshape(b * s, dg), dil_out_norm_w[l],
                          sb_out_norm_w[l], w_out[l], ffn_norm_w[l], w_gate[l], w_up[l],
                          w_down[l]).reshape(b, s, d)
    return x
```

```python
import functools

import jax
import jax.numpy as jnp
from jax import lax
from jax.experimental import pallas as pl
from jax.experimental.pallas import tpu as pltpu

F32 = jnp.float32
BF16 = jnp.bfloat16

HEAD_DIM = 64
HALF = HEAD_DIM // 2
LANES = 128
MXU_DIM = 256
PROJ_SUB = 512
N_BACK = 128
BLOCK = 128
DIL_GROUP = 2
SB_Q = 128
SB_K = MXU_DIM
SB_PAD = SB_K - SB_Q
SB_DEAD = 127.0
LOG2E = 1.4426950408889634
ROPE_THETA = 10000.0
EPS = 1e-6
VMEM_LIMIT = 56 * 1024 * 1024


def _dot(a, b):
    return jnp.dot(a, b, preferred_element_type=F32)


def _dot_nt(a, b):
    return lax.dot_general(a, b, (((1,), (1,)), ((), ())), preferred_element_type=F32)


def _rmsnorm(x, w):
    return x * lax.rsqrt(jnp.mean(x * x, axis=-1, keepdims=True) + EPS) * w


def _software_pipeline(n, stages, between=None):
    order = [0] + list(range(len(stages) - 1, 0, -1))
    for t in range(n + len(stages) - 1):
        for k in order:
            if 0 <= t - k < n:
                stages[k](t - k)
        if between is not None:
            between(t)


def _proj_kernel(x_ref, nw_ref, w_ref, g_ref, qw_ref, kw_ref, cos_ref, sin_ref,
                 qa_ref, ka_ref, va_ref, qs_ref, ks_ref, vs_ref):
    dg = qa_ref.shape[-1]
    lane = lax.broadcasted_iota(jnp.int32, (1, LANES), 1)
    first_half = (lane % HEAD_DIM) < HALF
    scale = HEAD_DIM ** -0.5

    def head_norm_rope(y, w_full, rows):
        cos = cos_ref[rows, :]
        sin = sin_ref[rows, :]
        outs = []
        for j in range(dg // MXU_DIM):
            yc = y[:, j * MXU_DIM:(j + 1) * MXU_DIM]
            ss = _dot((yc * yc).astype(BF16), g_ref[...])
            yn = yc * lax.rsqrt(ss * (1.0 / HEAD_DIM) + EPS) * w_full[:, j * MXU_DIM:(j + 1) * MXU_DIM]
            for i in range(MXU_DIM // LANES):
                v = yn[:, i * LANES:(i + 1) * LANES]
                partner = jnp.where(first_half, pltpu.roll(v, LANES - HALF, 1), pltpu.roll(v, HALF, 1))
                outs.append(v * cos + partner * sin)
        return jnp.concatenate(outs, axis=1)

    posts = ((qa_ref, lambda y, rows: head_norm_rope(y, qw_ref[...], rows) * scale),
             (ka_ref, lambda y, rows: head_norm_rope(y, kw_ref[...], rows)),
             (va_ref, lambda y, rows: y),
             (qs_ref, lambda y, rows: y * scale),
             (ks_ref, lambda y, rows: y),
             (vs_ref, lambda y, rows: y))

    n_parts = x_ref.shape[0] // PROJ_SUB
    hs = {}

    def normed(part):
        if part not in hs:
            hs[part] = _rmsnorm(x_ref[pl.ds(part * PROJ_SUB, PROJ_SUB), :], nw_ref[...]).astype(BF16)
        return hs[part]

    units = [(part, n) for part in range(n_parts) for n in range(len(posts))]
    matmul = lambda part, n: _dot(normed(part), w_ref[:, n * dg:(n + 1) * dg])
    ys = [matmul(*units[0])]
    for u, (part, n) in enumerate(units):
        if u + 1 < len(units):
            ys.append(matmul(*units[u + 1]))
        if n == 0 and part + 1 < n_parts:
            normed(part + 1)
        rows = pl.ds(part * PROJ_SUB, PROJ_SUB)
        ref, post = posts[n]
        ref[rows, :] = post(ys[u], rows).astype(BF16)


def _proj_call(x2, attn_norm_w, w_in, q_norm_w, k_norm_w, seq):
    t, d = x2.shape
    dg = w_in.shape[1] // 6
    tm = 2 * PROJ_SUB
    n_heads = dg // HEAD_DIM
    pos = jnp.arange(seq, dtype=F32)
    inv_freq = ROPE_THETA ** (-jnp.arange(0, HEAD_DIM, 2, dtype=F32) / HEAD_DIM)
    ang = pos[:, None] * inv_freq[None, :]
    cos_t = jnp.tile(jnp.cos(ang), (1, LANES // HALF))
    sin_t = jnp.tile(jnp.concatenate([-jnp.sin(ang), jnp.sin(ang)], axis=1), (1, LANES // HEAD_DIM))
    hid = jnp.arange(MXU_DIM) // HEAD_DIM
    g = (hid[:, None] == hid[None, :]).astype(BF16)
    qw = jnp.tile(q_norm_w.reshape(1, HEAD_DIM), (1, n_heads))
    kw = jnp.tile(k_norm_w.reshape(1, HEAD_DIM), (1, n_heads))
    tiles_per_seq = seq // tm
    row = lambda i: (i, 0)
    const = lambda i: (0, 0)
    out = jax.ShapeDtypeStruct((t, dg), BF16)
    return pl.pallas_call(
        _proj_kernel,
        grid=(t // tm,),
        in_specs=[
            pl.BlockSpec((tm, d), row),
            pl.BlockSpec((1, d), const),
            pl.BlockSpec(w_in.shape, const, pipeline_mode=pl.Buffered(1)),
            pl.BlockSpec(g.shape, const),
            pl.BlockSpec((1, dg), const),
            pl.BlockSpec((1, dg), const),
            pl.BlockSpec((tm, LANES), lambda i: (i % tiles_per_seq, 0)),
            pl.BlockSpec((tm, LANES), lambda i: (i % tiles_per_seq, 0)),
        ],
        out_specs=[pl.BlockSpec((tm, dg), row)] * 6,
        out_shape=[out] * 6,
        compiler_params=pltpu.CompilerParams(
            dimension_semantics=("arbitrary",), vmem_limit_bytes=VMEM_LIMIT),
        name="proj",
    )(x2, attn_norm_w.reshape(1, d), w_in.astype(BF16), g, qw, kw, cos_t, sin_t)


def _dilated_kernel(q_ref, k_ref, v_ref, o_ref, f_ref, f4_ref, q4_ref, k4_ref, v4_ref,
                    q16_ref, k16_ref, v16_ref, bias_ref, p_ref, md_ref, *state_refs):
    sa_refs, sb_refs = state_refs[:3], state_refs[3:]
    s = q_ref.shape[1]
    quarter, sixteenth = s // 4, s // 16
    n_blocks = s // BLOCK

    prep = []
    for n, (src, d4, d16) in enumerate(((q_ref, q4_ref, q16_ref), (k_ref, k4_ref, k16_ref),
                                        (v_ref, v4_ref, v16_ref))):
        def widen(c, n=n, src=src):
            rows = pl.ds(c * quarter, quarter)
            f_ref[n, rows, :] = src[0, rows, :].astype(F32)

        def by4(c, n=n, d4=d4):
            chunk = f_ref[n, pl.ds(c, quarter, stride=4), :]
            f4_ref[n, pl.ds(c * quarter, quarter), :] = chunk
            d4[pl.ds(c * quarter, quarter), :] = chunk.astype(BF16)

        def by16(c, n=n, d16=d16):
            chunk = f4_ref[n, pl.ds((c % 4) * quarter + c // 4, sixteenth, stride=4), :]
            d16[pl.ds(c * sixteenth, sixteenth), :] = chunk.astype(BF16)

        prep += [functools.partial(widen, c) for c in range(4)]
        prep += [functools.partial(by4, c) for c in range(4)]
        prep += [functools.partial(by16, c) for c in range(16)]

    lane = lax.broadcasted_iota(jnp.int32, (1, LANES), 1)
    head0 = lane < HEAD_DIM
    qi = lax.broadcasted_iota(jnp.int32, (2 * BLOCK, 2 * BLOCK), 0) % BLOCK
    kj = lax.broadcasted_iota(jnp.int32, (2 * BLOCK, 2 * BLOCK), 1)
    dist = qi + BLOCK - kj
    bias_ref[...] = jnp.where((dist >= 0) & (dist <= N_BACK), 0.0, -jnp.inf)

    def unstack(a):
        return jnp.where(head0, a[:BLOCK], a[BLOCK:])

    def fold(state, new):
        (acc_p, m_p, den_p), (acc, m, den) = state, new
        mx = jnp.maximum(m_p, m)
        w_p = jnp.exp2(m_p - mx)
        w_n = jnp.exp2(m - mx)
        return acc_p * w_p + acc * w_n, mx, den_p * w_p + den * w_n

    def keep1(start, new):
        for ref, val in zip(sa_refs, new):
            ref[pl.ds(start, BLOCK), :] = val

    def fold4(start, new):
        if start == 0:
            for c in range(4):
                for a_ref, b_ref in zip(sa_refs, sb_refs):
                    b_ref[pl.ds(c * quarter, quarter), :] = a_ref[pl.ds(c, quarter, stride=4), :]
        rows = pl.ds(start, BLOCK)
        for ref, val in zip(sb_refs, fold([ref[rows, :] for ref in sb_refs], new)):
            ref[rows, :] = val

    def finish16(start, new):
        if start == 0:
            for c in range(16):
                src = pl.ds((c % 4) * quarter + c // 4, sixteenth, stride=4)
                for a_ref, b_ref in zip(sa_refs, sb_refs):
                    a_ref[pl.ds(c * sixteenth, sixteenth), :] = b_ref[src, :]
        acc, _, den = fold([ref[pl.ds(start, BLOCK), :] for ref in sa_refs], new)
        sb_refs[0][pl.ds(start, BLOCK), :] = acc / den

    nat = lambda ref: ref.at[0]
    branches = ((nat(q_ref), nat(k_ref), nat(v_ref), n_blocks, keep1),
                (q4_ref, k4_ref, v4_ref, quarter // BLOCK, fold4),
                (q16_ref, k16_ref, v16_ref, sixteenth // BLOCK, finish16))
    groups_per_branch = n_blocks // DIL_GROUP
    n_groups = len(branches) * groups_per_branch

    def blocks_of(g):
        branch = branches[g // groups_per_branch]
        out = []
        for j in range(DIL_GROUP):
            blk = (g % groups_per_branch) * DIL_GROUP + j
            start = blk * BLOCK
            first = blk % branch[3] == 0
            out.append((branch, start, (start, BLOCK) if first else (start - BLOCK, 2 * BLOCK), first))
        return out

    def stage_softmax(g, slot):
        for j, ((q_src, k_src, _, _, _), start, (k0, kn), first) in enumerate(blocks_of(g)):
            q = q_src[pl.ds(start, BLOCK), :]
            zero = jnp.zeros_like(q)
            qs = jnp.concatenate([jnp.where(head0, q, zero), jnp.where(head0, zero, q)], axis=0)
            bias = bias_ref[:, BLOCK:] if first else bias_ref[...]
            sc = _dot_nt(qs, k_src[pl.ds(k0, kn), :]) * LOG2E + bias
            m = jnp.max(sc, axis=-1, keepdims=True)
            p = jnp.exp2(sc - m)
            p_ref[slot, j, :, pl.ds(0, kn)] = p.astype(BF16)
            md_ref[slot, j, 0] = unstack(m)
            md_ref[slot, j, 1] = unstack(jnp.sum(p, axis=-1, keepdims=True))

    def stage_values(g, slot):
        for j, ((_, _, v_src, _, post), start, (k0, kn), _) in enumerate(blocks_of(g)):
            acc = unstack(_dot(p_ref[slot, j, :, pl.ds(0, kn)], v_src[pl.ds(k0, kn), :]))
            post(start, (acc, md_ref[slot, j, 0], md_ref[slot, j, 1]))

    per_step = -(-len(prep) // (groups_per_branch - 2))

    def between(step):
        for piece in prep[step * per_step:(step + 1) * per_step]:
            piece()

    _software_pipeline(n_groups, [lambda g, stage=stage: stage(g, g % 2)
                                  for stage in (stage_softmax, stage_values)], between)

    o16_ref, o4_ref = sb_refs[0], sa_refs[0]
    for c in range(16):
        o4_ref[pl.ds((c % 4) * quarter + c // 4, sixteenth, stride=4), :] = (
            o16_ref[pl.ds(c * sixteenth, sixteenth), :])
    for c in range(4):
        o_ref[0, pl.ds(c, quarter, stride=4), :] = o4_ref[pl.ds(c * quarter, quarter), :]


def _dilated_call(qa, ka, va):
    b, s, dg = qa.shape
    pairs = dg // LANES
    spec = pl.BlockSpec((1, s, LANES), lambda bi, hp: (bi, 0, hp))
    slab = lambda dt: pltpu.VMEM((s, LANES), dt)
    return pl.pallas_call(
        _dilated_kernel,
        grid=(b, pairs),
        in_specs=[spec] * 3,
        out_specs=spec,
        out_shape=jax.ShapeDtypeStruct((b, s, dg), F32),
        scratch_shapes=([pltpu.VMEM((3, s, LANES), F32)] * 2 + [slab(BF16)] * 6 + [
                        pltpu.VMEM((2 * BLOCK, 2 * BLOCK), F32),
                        pltpu.VMEM((2, DIL_GROUP, 2 * BLOCK, 2 * BLOCK), BF16),
                        pltpu.VMEM((2, DIL_GROUP, 2, BLOCK, LANES), F32)] + [slab(F32)] * 6),
        compiler_params=pltpu.CompilerParams(
            dimension_semantics=("arbitrary",) * 2, vmem_limit_bytes=VMEM_LIMIT),
        name="dilated",
    )(qa, ka, va)


def _sb_kernel(q_ref, k_ref, v_ref, u_ref, o_ref, kp_ref, vp_ref, z_ref, w_ref, shift_ref, live_ref):
    s = q_ref.shape[1]
    kp_ref[pl.ds(0, SB_PAD), :] = jnp.zeros((SB_PAD, LANES), BF16)
    vp_ref[pl.ds(0, SB_PAD), :] = jnp.zeros((SB_PAD, LANES), BF16)
    kp_ref[pl.ds(SB_PAD, s), :] = k_ref[0]
    vp_ref[pl.ds(SB_PAD, s), :] = v_ref[0]

    lane = lax.broadcasted_iota(jnp.int32, (1, LANES), 1)
    head0 = lane < HEAD_DIM
    ri = lax.broadcasted_iota(jnp.int32, (2 * SB_Q, SB_K), 0) % SB_Q
    cj = lax.broadcasted_iota(jnp.int32, (2 * SB_Q, SB_K), 1)
    causal = cj < ri + (SB_K - SB_Q)
    u = u_ref[...]

    def neg_log_keep(z, mask):
        nk = jnp.maximum(z, 0.0) + jnp.log2(1.0 + jnp.exp2(-jnp.abs(z)))
        return nk if mask is None else jnp.where(mask, nk, 0.0)

    def inclusive_suffix(nk):
        hi = nk.astype(BF16)
        lo = (nk - hi.astype(F32)).astype(BF16)
        cs = _dot(jnp.concatenate([hi, lo], axis=0), u)
        return cs[:2 * SB_Q] + cs[2 * SB_Q:]

    def weights(z, cs, shift, mask):
        x = z - cs
        if shift is not None:
            x = x - shift
        a = jnp.exp2(x)
        if mask is not None:
            a = jnp.where(mask, a, 0.0)
        return a.astype(BF16)

    def live(shift):
        return jnp.min(shift) < SB_DEAD

    def block_start(qb):
        return qb * SB_Q if isinstance(qb, int) else pl.multiple_of(qb * SB_Q, SB_Q)

    def stacked_q(qb):
        q = q_ref[0, pl.ds(block_start(qb), SB_Q), :]
        zero = jnp.zeros_like(q)
        return jnp.concatenate([jnp.where(head0, q, zero), jnp.where(head0, zero, q)], axis=0)

    def tile_keys(qb, i):
        return pl.ds(block_start(qb - 1 + SB_PAD // SB_Q - 2 * i), SB_K)

    def score(qb, i):
        return _dot_nt(stacked_q(qb), kp_ref[tile_keys(qb, i), :])

    def unstack(acc):
        return jnp.where(head0, acc[:SB_Q], acc[SB_Q:])

    def stage_scores(qb):
        z_ref[qb % 2] = score(qb, 0)

    def stage_weights(qb):
        z = z_ref[qb % 2] * LOG2E
        nk = neg_log_keep(z, causal)
        w_ref[qb % 2] = weights(z, inclusive_suffix(nk), None, causal)
        shift = jnp.sum(nk, axis=-1, keepdims=True)
        shift_ref[qb] = shift
        live_ref[qb] = live(shift).astype(jnp.int32)

    def stage_values(qb):
        o_ref[0, pl.ds(qb * SB_Q, SB_Q), :] = unstack(_dot(w_ref[qb % 2], vp_ref[tile_keys(qb, 0), :]))

    _software_pipeline(s // SB_Q, [stage_scores, stage_weights, stage_values])

    def tail(qb):
        def cond(state):
            return (state[0] <= qb // 2) & state[1]

        def body(state):
            i, _, acc, shift = state
            z = score(qb, i) * LOG2E
            nk = neg_log_keep(z, None)
            w = weights(z, inclusive_suffix(nk), shift, None)
            shift = shift + jnp.sum(nk, axis=-1, keepdims=True)
            return i + 1, live(shift), acc + _dot(w, vp_ref[tile_keys(qb, i), :]), shift

        state = (jnp.int32(1), jnp.bool_(True), jnp.zeros((2 * SB_Q, LANES), F32), shift_ref[qb])
        o_ref[0, pl.ds(block_start(qb), SB_Q), :] += unstack(lax.while_loop(cond, body, state)[2])

    def tail_body(qb, carry):
        @pl.when(live_ref[qb] != 0)
        def _():
            tail(qb)
        return carry
    lax.fori_loop(0, s // SB_Q, tail_body, 0)


def _sb_call(qs, ks, vs):
    b, s, dg = qs.shape
    pairs = dg // LANES
    idx = jnp.arange(SB_K)
    u = (idx[:, None] >= idx[None, :]).astype(BF16)
    spec = pl.BlockSpec((1, s, LANES), lambda bi, hp: (bi, 0, hp))
    return pl.pallas_call(
        _sb_kernel,
        grid=(b, pairs),
        in_specs=[spec, spec, spec, pl.BlockSpec((SB_K, SB_K), lambda bi, hp: (0, 0))],
        out_specs=spec,
        out_shape=jax.ShapeDtypeStruct((b, s, dg), F32),
        scratch_shapes=[pltpu.VMEM((s + SB_PAD, LANES), BF16)] * 2 + [
            pltpu.VMEM((2, 2 * SB_Q, SB_K), F32),
            pltpu.VMEM((2, 2 * SB_Q, SB_K), BF16),
            pltpu.VMEM((s // SB_Q, 2 * SB_Q, 1), F32),
            pltpu.SMEM((s // SB_Q,), jnp.int32)],
        compiler_params=pltpu.CompilerParams(
            dimension_semantics=("arbitrary",) * 2, vmem_limit_bytes=VMEM_LIMIT),
        name="stick_breaking",
    )(qs, ks, vs, u)


def _out_ffn_kernel(x_ref, od_ref, os_ref, dw_ref, sw_ref, wo_ref, fw_ref, wg_ref, wu_ref, wd_ref,
                    out_ref, x1_ref, act_ref, *, f_chunk):
    dg = od_ref.shape[-1]
    nd = _rmsnorm(od_ref[...], dw_ref[...]).astype(BF16)
    ns = _rmsnorm(os_ref[...], sw_ref[...]).astype(BF16)
    x1 = x_ref[...] + _dot(nd, wo_ref[:dg, :]) + _dot(ns, wo_ref[dg:, :])
    h = _rmsnorm(x1, fw_ref[...]).astype(BF16)
    x1_ref[...] = x1

    def gate_up(i):
        cols = slice(i * f_chunk, (i + 1) * f_chunk)
        return _dot(h, wg_ref[:, cols]), _dot(h, wu_ref[:, cols])

    n_chunks = wg_ref.shape[1] // f_chunk
    nxt = gate_up(0)
    for i in range(n_chunks):
        g, up = nxt
        if i + 1 < n_chunks:
            nxt = gate_up(i + 1)
        act_ref[:, i * f_chunk:(i + 1) * f_chunk] = (g / (1.0 + jnp.exp(-g)) * up).astype(BF16)
    out_ref[...] = x1_ref[...] + _dot(act_ref[...], wd_ref[...])


def _out_ffn_call(x2, o_dil, o_sb, dil_w, sb_w, w_out, ffn_w, w_gate, w_up, w_down):
    t, d = x2.shape
    dg = o_dil.shape[1]
    d_ff = w_gate.shape[1]
    tm = 512
    f_chunk = 256
    row = lambda i: (i, 0)
    const = lambda i: (0, 0)
    resident = lambda shape: pl.BlockSpec(shape, const, pipeline_mode=pl.Buffered(1))
    return pl.pallas_call(
        functools.partial(_out_ffn_kernel, f_chunk=f_chunk),
        grid=(t // tm,),
        in_specs=[
            pl.BlockSpec((tm, d), row),
            pl.BlockSpec((tm, dg), row),
            pl.BlockSpec((tm, dg), row),
            pl.BlockSpec((1, dg), const),
            pl.BlockSpec((1, dg), const),
            resident((2 * dg, d)),
            pl.BlockSpec((1, d), const),
            resident((d, d_ff)),
            resident((d, d_ff)),
            resident((d_ff, d)),
        ],
        out_specs=pl.BlockSpec((tm, d), row),
        out_shape=jax.ShapeDtypeStruct((t, d), F32),
        scratch_shapes=[pltpu.VMEM((tm, d), F32), pltpu.VMEM((tm, d_ff), BF16)],
        compiler_params=pltpu.CompilerParams(
            dimension_semantics=("arbitrary",), vmem_limit_bytes=VMEM_LIMIT),
        name="out_ffn",
    )(x2, o_dil, o_sb, dil_w.reshape(1, dg), sb_w.reshape(1, dg), w_out.astype(BF16),
      ffn_w.reshape(1, d), w_gate.astype(BF16), w_up.astype(BF16), w_down.astype(BF16))


def kernel(x, attn_norm_w, w_in, q_norm_w, k_norm_w, dil_out_norm_w, sb_out_norm_w, w_out,
           ffn_norm_w, w_gate, w_up, w_down):
    b, s, d = x.shape
    for l in range(attn_norm_w.shape[0]):
        x2 = x.reshape(b * s, d)
        qa, ka, va, qs, ks, vs = _proj_call(x2, attn_norm_w[l], w_in[l], q_norm_w[l], k_norm_w[l], s)
        dg = qa.shape[1]
        qa, ka, va, qs, ks, vs = (a.reshape(b, s, dg) for a in (qa, ka, va, qs, ks, vs))

        o_dil = _dilated_call(qa, ka, va)
        o_sb = _sb_call(qs, ks, vs)

        x = _out_ffn_call(x2, o_dil.reshape(b * s, dg), o_sb.reshape(b * s, dg), dil_out_norm_w[l],
                          sb_out_norm_w[l], w_out[l], ffn_norm_w[l], w_gate[l], w_up[l],
                          w_down[l]).reshape(b, s, d)
    return x
```

```python
import functools

import jax
import jax.numpy as jnp
from jax import lax
from jax.experimental import pallas as pl
from jax.experimental.pallas import tpu as pltpu

F32 = jnp.float32
BF16 = jnp.bfloat16

HEAD_DIM = 64
HALF = HEAD_DIM // 2
LANES = 128
MXU_DIM = 256
PROJ_SUB = 512
N_BACK = 128
BLOCK = 128
DIL_GROUP = 1
SB_Q = 128
SB_K = MXU_DIM
SB_PAD = SB_K - SB_Q
SB_DEAD = 127.0
LOG2E = 1.4426950408889634
ROPE_THETA = 10000.0
EPS = 1e-6
VMEM_LIMIT = 56 * 1024 * 1024


def _dot(a, b):
    return jnp.dot(a, b, preferred_element_type=F32)


def _dot_nt(a, b):
    return lax.dot_general(a, b, (((1,), (1,)), ((), ())), preferred_element_type=F32)


def _rmsnorm(x, w):
    return x * lax.rsqrt(jnp.mean(x * x, axis=-1, keepdims=True) + EPS) * w


def _software_pipeline(n, stages, between=None):
    order = [0] + list(range(len(stages) - 1, 0, -1))
    for t in range(n + len(stages) - 1):
        for k in order:
            if 0 <= t - k < n:
                stages[k](t - k)
        if between is not None:
            between(t)


def _proj_kernel(x_ref, nw_ref, w_ref, g_ref, qw_ref, kw_ref, cos_ref, sin_ref,
                 qa_ref, ka_ref, va_ref, qs_ref, ks_ref, vs_ref):
    dg = qa_ref.shape[-1]
    lane = lax.broadcasted_iota(jnp.int32, (1, LANES), 1)
    first_half = (lane % HEAD_DIM) < HALF
    scale = HEAD_DIM ** -0.5

    def head_norm_rope(y, w_full, rows):
        cos = cos_ref[rows, :]
        sin = sin_ref[rows, :]
        outs = []
        for j in range(dg // MXU_DIM):
            yc = y[:, j * MXU_DIM:(j + 1) * MXU_DIM]
            ss = _dot((yc * yc).astype(BF16), g_ref[...])
            yn = yc * lax.rsqrt(ss * (1.0 / HEAD_DIM) + EPS) * w_full[:, j * MXU_DIM:(j + 1) * MXU_DIM]
            for i in range(MXU_DIM // LANES):
                v = yn[:, i * LANES:(i + 1) * LANES]
                partner = jnp.where(first_half, pltpu.roll(v, LANES - HALF, 1), pltpu.roll(v, HALF, 1))
                outs.append(v * cos + partner * sin)
        return jnp.concatenate(outs, axis=1)

    posts = ((qa_ref, lambda y, rows: head_norm_rope(y, qw_ref[...], rows) * scale),
             (ka_ref, lambda y, rows: head_norm_rope(y, kw_ref[...], rows)),
             (va_ref, lambda y, rows: y),
             (qs_ref, lambda y, rows: y * scale),
             (ks_ref, lambda y, rows: y),
             (vs_ref, lambda y, rows: y))

    n_parts = x_ref.shape[0] // PROJ_SUB
    hs = {}

    def normed(part):
        if part not in hs:
            hs[part] = _rmsnorm(x_ref[pl.ds(part * PROJ_SUB, PROJ_SUB), :], nw_ref[...]).astype(BF16)
        return hs[part]

    units = [(part, n) for part in range(n_parts) for n in range(len(posts))]
    matmul = lambda part, n: _dot(normed(part), w_ref[:, n * dg:(n + 1) * dg])
    ys = [matmul(*units[0])]
    for u, (part, n) in enumerate(units):
        if u + 1 < len(units):
            ys.append(matmul(*units[u + 1]))
        if n == 0 and part + 1 < n_parts:
            normed(part + 1)
        rows = pl.ds(part * PROJ_SUB, PROJ_SUB)
        ref, post = posts[n]
        ref[rows, :] = post(ys[u], rows).astype(BF16)


def _proj_call(x2, attn_norm_w, w_in, q_norm_w, k_norm_w, seq):
    t, d = x2.shape
    dg = w_in.shape[1] // 6
    tm = 2 * PROJ_SUB
    n_heads = dg // HEAD_DIM
    pos = jnp.arange(seq, dtype=F32)
    inv_freq = ROPE_THETA ** (-jnp.arange(0, HEAD_DIM, 2, dtype=F32) / HEAD_DIM)
    ang = pos[:, None] * inv_freq[None, :]
    cos_t = jnp.tile(jnp.cos(ang), (1, LANES // HALF))
    sin_t = jnp.tile(jnp.concatenate([-jnp.sin(ang), jnp.sin(ang)], axis=1), (1, LANES // HEAD_DIM))
    hid = jnp.arange(MXU_DIM) // HEAD_DIM
    g = (hid[:, None] == hid[None, :]).astype(BF16)
    qw = jnp.tile(q_norm_w.reshape(1, HEAD_DIM), (1, n_heads))
    kw = jnp.tile(k_norm_w.reshape(1, HEAD_DIM), (1, n_heads))
    tiles_per_seq = seq // tm
    row = lambda i: (i, 0)
    const = lambda i: (0, 0)
    out = jax.ShapeDtypeStruct((t, dg), BF16)
    return pl.pallas_call(
        _proj_kernel,
        grid=(t // tm,),
        in_specs=[
            pl.BlockSpec((tm, d), row),
            pl.BlockSpec((1, d), const),
            pl.BlockSpec(w_in.shape, const, pipeline_mode=pl.Buffered(1)),
            pl.BlockSpec(g.shape, const),
            pl.BlockSpec((1, dg), const),
            pl.BlockSpec((1, dg), const),
            pl.BlockSpec((tm, LANES), lambda i: (i % tiles_per_seq, 0)),
            pl.BlockSpec((tm, LANES), lambda i: (i % tiles_per_seq, 0)),
        ],
        out_specs=[pl.BlockSpec((tm, dg), row)] * 6,
        out_shape=[out] * 6,
        compiler_params=pltpu.CompilerParams(
            dimension_semantics=("arbitrary",), vmem_limit_bytes=VMEM_LIMIT),
        name="proj",
    )(x2, attn_norm_w.reshape(1, d), w_in.astype(BF16), g, qw, kw, cos_t, sin_t)


def _dilated_kernel(q_ref, k_ref, v_ref, o_ref, f_ref, f4_ref, q4_ref, k4_ref, v4_ref,
                    q16_ref, k16_ref, v16_ref, bias_ref, p_ref, md_ref, *state_refs):
    sa_refs, sb_refs = state_refs[:3], state_refs[3:]
    s = q_ref.shape[1]
    quarter, sixteenth = s // 4, s // 16
    n_blocks = s // BLOCK

    prep, prep16 = [], []
    for n, (src, d4, d16) in enumerate(((q_ref, q4_ref, q16_ref), (k_ref, k4_ref, k16_ref),
                                        (v_ref, v4_ref, v16_ref))):
        def widen(c, n=n, src=src):
            rows = pl.ds(c * quarter, quarter)
            f_ref[n, rows, :] = src[0, rows, :].astype(F32)

        def by4(c, n=n, d4=d4):
            chunk = f_ref[n, pl.ds(c, quarter, stride=4), :]
            f4_ref[n, pl.ds(c * quarter, quarter), :] = chunk
            d4[pl.ds(c * quarter, quarter), :] = chunk.astype(BF16)

        def by16(c, n=n, d16=d16):
            chunk = f4_ref[n, pl.ds((c % 4) * quarter + c // 4, sixteenth, stride=4), :]
            d16[pl.ds(c * sixteenth, sixteenth), :] = chunk.astype(BF16)

        prep += [functools.partial(widen, c) for c in range(4)]
        prep += [functools.partial(by4, c) for c in range(4)]
        prep16 += [functools.partial(by16, c) for c in range(16)]

    lane = lax.broadcasted_iota(jnp.int32, (1, LANES), 1)
    head0 = lane < HEAD_DIM
    qi = lax.broadcasted_iota(jnp.int32, (2 * BLOCK, 2 * BLOCK), 0) % BLOCK
    kj = lax.broadcasted_iota(jnp.int32, (2 * BLOCK, 2 * BLOCK), 1)
    dist = qi + BLOCK - kj
    bias_ref[...] = jnp.where((dist >= 0) & (dist <= N_BACK), 0.0, -jnp.inf)

    def unstack(a):
        return jnp.where(head0, a[:BLOCK], a[BLOCK:])

    def fold(state, new):
        (acc_p, m_p, den_p), (acc, m, den) = state, new
        mx = jnp.maximum(m_p, m)
        w_p = jnp.exp2(m_p - mx)
        w_n = jnp.exp2(m - mx)
        return acc_p * w_p + acc * w_n, mx, den_p * w_p + den * w_n

    def keep1(start, new):
        for ref, val in zip(sa_refs, new):
            ref[pl.ds(start, BLOCK), :] = val

    def fold4(start, new):
        if start == 0:
            for c in range(4):
                for a_ref, b_ref in zip(sa_refs, sb_refs):
                    b_ref[pl.ds(c * quarter, quarter), :] = a_ref[pl.ds(c, quarter, stride=4), :]
        rows = pl.ds(start, BLOCK)
        for ref, val in zip(sb_refs, fold([ref[rows, :] for ref in sb_refs], new)):
            ref[rows, :] = val

    def finish16(start, new):
        if start == 0:
            for c in range(16):
                src = pl.ds((c % 4) * quarter + c // 4, sixteenth, stride=4)
                for a_ref, b_ref in zip(sa_refs, sb_refs):
                    a_ref[pl.ds(c * sixteenth, sixteenth), :] = b_ref[src, :]
        acc, _, den = fold([ref[pl.ds(start, BLOCK), :] for ref in sa_refs], new)
        sb_refs[0][pl.ds(start, BLOCK), :] = acc / den

    nat = lambda ref: ref.at[0]
    branches = ((nat(q_ref), nat(k_ref), nat(v_ref), n_blocks, keep1),
                (q4_ref, k4_ref, v4_ref, quarter // BLOCK, fold4),
                (q16_ref, k16_ref, v16_ref, sixteenth // BLOCK, finish16))
    groups_per_branch = n_blocks // DIL_GROUP
    n_groups = len(branches) * groups_per_branch

    def blocks_of(g):
        branch = branches[g // groups_per_branch]
        out = []
        for j in range(DIL_GROUP):
            blk = (g % groups_per_branch) * DIL_GROUP + j
            start = blk * BLOCK
            first = blk % branch[3] == 0
            out.append((branch, start, (start, BLOCK) if first else (start - BLOCK, 2 * BLOCK), first))
        return out

    def stage_softmax(g, slot):
        for j, ((q_src, k_src, _, _, _), start, (k0, kn), first) in enumerate(blocks_of(g)):
            q = q_src[pl.ds(start, BLOCK), :]
            zero = jnp.zeros_like(q)
            qs = jnp.concatenate([jnp.where(head0, q, zero), jnp.where(head0, zero, q)], axis=0)
            bias = bias_ref[:, BLOCK:] if first else bias_ref[...]
            sc = _dot_nt(qs, k_src[pl.ds(k0, kn), :]) * LOG2E + bias
            m = jnp.max(sc, axis=-1, keepdims=True)
            p = jnp.exp2(sc - m)
            p_ref[slot, j, :, pl.ds(0, kn)] = p.astype(BF16)
            md_ref[slot, j, 0] = unstack(m)
            md_ref[slot, j, 1] = unstack(jnp.sum(p, axis=-1, keepdims=True))

    def stage_values(g, slot):
        for j, ((_, _, v_src, _, post), start, (k0, kn), _) in enumerate(blocks_of(g)):
            acc = unstack(_dot(p_ref[slot, j, :, pl.ds(0, kn)], v_src[pl.ds(k0, kn), :]))
            post(start, (acc, md_ref[slot, j, 0], md_ref[slot, j, 1]))

    def between(step):
        pieces = (prep, prep16)[step // groups_per_branch] if step < 2 * groups_per_branch else []
        per_step = -(-len(pieces) // (groups_per_branch - 2))
        local = step % groups_per_branch
        for piece in pieces[local * per_step:(local + 1) * per_step]:
            piece()

    _software_pipeline(n_groups, [lambda g, stage=stage: stage(g, g % 2)
                                  for stage in (stage_softmax, stage_values)], between)

    o16_ref, o4_ref = sb_refs[0], sa_refs[0]
    for c in range(16):
        o4_ref[pl.ds((c % 4) * quarter + c // 4, sixteenth, stride=4), :] = (
            o16_ref[pl.ds(c * sixteenth, sixteenth), :])
    for c in range(4):
        o_ref[0, pl.ds(c, quarter, stride=4), :] = o4_ref[pl.ds(c * quarter, quarter), :]


def _dilated_call(qa, ka, va):
    b, s, dg = qa.shape
    pairs = dg // LANES
    spec = pl.BlockSpec((1, s, LANES), lambda bi, hp: (bi, 0, hp))
    slab = lambda dt: pltpu.VMEM((s, LANES), dt)
    return pl.pallas_call(
        _dilated_kernel,
        grid=(b, pairs),
        in_specs=[spec] * 3,
        out_specs=spec,
        out_shape=jax.ShapeDtypeStruct((b, s, dg), F32),
        scratch_shapes=([pltpu.VMEM((3, s, LANES), F32)] * 2 + [slab(BF16)] * 6 + [
                        pltpu.VMEM((2 * BLOCK, 2 * BLOCK), F32),
                        pltpu.VMEM((2, DIL_GROUP, 2 * BLOCK, 2 * BLOCK), BF16),
                        pltpu.VMEM((2, DIL_GROUP, 2, BLOCK, LANES), F32)] + [slab(F32)] * 6),
        compiler_params=pltpu.CompilerParams(
            dimension_semantics=("arbitrary",) * 2, vmem_limit_bytes=VMEM_LIMIT),
        name="dilated",
    )(qa, ka, va)


def _sb_kernel(q_ref, k_ref, v_ref, u_ref, o_ref, kp_ref, vp_ref, z_ref, w_ref, shift_ref, live_ref):
    s = q_ref.shape[1]
    kp_ref[pl.ds(0, SB_PAD), :] = jnp.zeros((SB_PAD, LANES), BF16)
    vp_ref[pl.ds(0, SB_PAD), :] = jnp.zeros((SB_PAD, LANES), BF16)
    kp_ref[pl.ds(SB_PAD, s), :] = k_ref[0]
    vp_ref[pl.ds(SB_PAD, s), :] = v_ref[0]

    lane = lax.broadcasted_iota(jnp.int32, (1, LANES), 1)
    head0 = lane < HEAD_DIM
    ri = lax.broadcasted_iota(jnp.int32, (2 * SB_Q, SB_K), 0) % SB_Q
    cj = lax.broadcasted_iota(jnp.int32, (2 * SB_Q, SB_K), 1)
    causal = cj < ri + (SB_K - SB_Q)
    u = u_ref[...]

    def neg_log_keep(z, mask):
        nk = jnp.maximum(z, 0.0) + jnp.log2(1.0 + jnp.exp2(-jnp.abs(z)))
        return nk if mask is None else jnp.where(mask, nk, 0.0)

    def inclusive_suffix(nk):
        hi = nk.astype(BF16)
        lo = (nk - hi.astype(F32)).astype(BF16)
        cs = _dot(jnp.concatenate([hi, lo], axis=0), u)
        return cs[:2 * SB_Q] + cs[2 * SB_Q:]

    def weights(z, cs, shift, mask):
        x = z - cs
        if shift is not None:
            x = x - shift
        a = jnp.exp2(x)
        if mask is not None:
            a = jnp.where(mask, a, 0.0)
        return a.astype(BF16)

    def live(shift):
        return jnp.min(shift) < SB_DEAD

    def block_start(qb):
        return qb * SB_Q if isinstance(qb, int) else pl.multiple_of(qb * SB_Q, SB_Q)

    def stacked_q(qb):
        q = q_ref[0, pl.ds(block_start(qb), SB_Q), :]
        zero = jnp.zeros_like(q)
        return jnp.concatenate([jnp.where(head0, q, zero), jnp.where(head0, zero, q)], axis=0)

    def tile_keys(qb, i):
        return pl.ds(block_start(qb - 1 + SB_PAD // SB_Q - 2 * i), SB_K)

    def score(qb, i):
        return _dot_nt(stacked_q(qb), kp_ref[tile_keys(qb, i), :])

    def unstack(acc):
        return jnp.where(head0, acc[:SB_Q], acc[SB_Q:])

    def stage_scores(qb):
        z_ref[qb % 2] = score(qb, 0)

    def stage_weights(qb):
        z = z_ref[qb % 2] * LOG2E
        nk = neg_log_keep(z, causal)
        w_ref[qb % 2] = weights(z, inclusive_suffix(nk), None, causal)
        shift = jnp.sum(nk, axis=-1, keepdims=True)
        shift_ref[qb] = shift
        live_ref[qb] = live(shift).astype(jnp.int32)

    def stage_values(qb):
        o_ref[0, pl.ds(qb * SB_Q, SB_Q), :] = unstack(_dot(w_ref[qb % 2], vp_ref[tile_keys(qb, 0), :]))

    _software_pipeline(s // SB_Q, [stage_scores, stage_weights, stage_values])

    def tail(qb):
        def cond(state):
            return (state[0] <= qb // 2) & state[1]

        def body(state):
            i, _, acc, shift = state
            z = score(qb, i) * LOG2E
            nk = neg_log_keep(z, None)
            w = weights(z, inclusive_suffix(nk), shift, None)
            shift = shift + jnp.sum(nk, axis=-1, keepdims=True)
            return i + 1, live(shift), acc + _dot(w, vp_ref[tile_keys(qb, i), :]), shift

        state = (jnp.int32(1), jnp.bool_(True), jnp.zeros((2 * SB_Q, LANES), F32), shift_ref[qb])
        o_ref[0, pl.ds(block_start(qb), SB_Q), :] += unstack(lax.while_loop(cond, body, state)[2])

    def tail_body(qb, carry):
        @pl.when(live_ref[qb] != 0)
        def _():
            tail(qb)
        return carry
    lax.fori_loop(0, s // SB_Q, tail_body, 0)


def _sb_call(qs, ks, vs):
    b, s, dg = qs.shape
    pairs = dg // LANES
    idx = jnp.arange(SB_K)
    u = (idx[:, None] >= idx[None, :]).astype(BF16)
    spec = pl.BlockSpec((1, s, LANES), lambda bi, hp: (bi, 0, hp))
    return pl.pallas_call(
        _sb_kernel,
        grid=(b, pairs),
        in_specs=[spec, spec, spec, pl.BlockSpec((SB_K, SB_K), lambda bi, hp: (0, 0))],
        out_specs=spec,
        out_shape=jax.ShapeDtypeStruct((b, s, dg), F32),
        scratch_shapes=[pltpu.VMEM((s + SB_PAD, LANES), BF16)] * 2 + [
            pltpu.VMEM((2, 2 * SB_Q, SB_K), F32),
            pltpu.VMEM((2, 2 * SB_Q, SB_K), BF16),
            pltpu.VMEM((s // SB_Q, 2 * SB_Q, 1), F32),
            pltpu.SMEM((s // SB_Q,), jnp.int32)],
        compiler_params=pltpu.CompilerParams(
            dimension_semantics=("arbitrary",) * 2, vmem_limit_bytes=VMEM_LIMIT),
        name="stick_breaking",
    )(qs, ks, vs, u)


def _out_ffn_kernel(x_ref, od_ref, os_ref, dw_ref, sw_ref, wo_ref, fw_ref, wg_ref, wu_ref, wd_ref,
                    out_ref, x1_ref, act_ref, *, f_chunk):
    dg = od_ref.shape[-1]
    nd = _rmsnorm(od_ref[...], dw_ref[...]).astype(BF16)
    ns = _rmsnorm(os_ref[...], sw_ref[...]).astype(BF16)
    x1 = x_ref[...] + _dot(nd, wo_ref[:dg, :]) + _dot(ns, wo_ref[dg:, :])
    h = _rmsnorm(x1, fw_ref[...]).astype(BF16)
    x1_ref[...] = x1

    def gate_up(i):
        cols = slice(i * f_chunk, (i + 1) * f_chunk)
        return _dot(h, wg_ref[:, cols]), _dot(h, wu_ref[:, cols])

    n_chunks = wg_ref.shape[1] // f_chunk
    nxt = gate_up(0)
    for i in range(n_chunks):
        g, up = nxt
        if i + 1 < n_chunks:
            nxt = gate_up(i + 1)
        act_ref[:, i * f_chunk:(i + 1) * f_chunk] = (g / (1.0 + jnp.exp(-g)) * up).astype(BF16)
    out_ref[...] = x1_ref[...] + _dot(act_ref[...], wd_ref[...])


def _out_ffn_call(x2, o_dil, o_sb, dil_w, sb_w, w_out, ffn_w, w_gate, w_up, w_down):
    t, d = x2.shape
    dg = o_dil.shape[1]
    d_ff = w_gate.shape[1]
    tm = 512
    f_chunk = 256
    row = lambda i: (i, 0)
    const = lambda i: (0, 0)
    resident = lambda shape: pl.BlockSpec(shape, const, pipeline_mode=pl.Buffered(1))
    return pl.pallas_call(
        functools.partial(_out_ffn_kernel, f_chunk=f_chunk),
        grid=(t // tm,),
        in_specs=[
            pl.BlockSpec((tm, d), row),
            pl.BlockSpec((tm, dg), row),
            pl.BlockSpec((tm, dg), row),
            pl.BlockSpec((1, dg), const),
            pl.BlockSpec((1, dg), const),
            resident((2 * dg, d)),
            pl.BlockSpec((1, d), const),
            resident((d, d_ff)),
            resident((d, d_ff)),
            resident((d_ff, d)),
        ],
        out_specs=pl.BlockSpec((tm, d), row),
        out_shape=jax.ShapeDtypeStruct((t, d), F32),
        scratch_shapes=[pltpu.VMEM((tm, d), F32), pltpu.VMEM((tm, d_ff), BF16)],
        compiler_params=pltpu.CompilerParams(
            dimension_semantics=("arbitrary",), vmem_limit_bytes=VMEM_LIMIT),
        name="out_ffn",
    )(x2, o_dil, o_sb, dil_w.reshape(1, dg), sb_w.reshape(1, dg), w_out.astype(BF16),
      ffn_w.reshape(1, d), w_gate.astype(BF16), w_up.astype(BF16), w_down.astype(BF16))


def kernel(x, attn_norm_w, w_in, q_norm_w, k_norm_w, dil_out_norm_w, sb_out_norm_w, w_out,
           ffn_norm_w, w_gate, w_up, w_down):
    b, s, d = x.shape
    for l in range(attn_norm_w.shape[0]):
        x2 = x.reshape(b * s, d)
        qa, ka, va, qs, ks, vs = _proj_call(x2, attn_norm_w[l], w_in[l], q_norm_w[l], k_norm_w[l], s)
        dg = qa.shape[1]
        qa, ka, va, qs, ks, vs = (a.reshape(b, s, dg) for a in (qa, ka, va, qs, ks, vs))

        o_dil = _dilated_call(qa, ka, va)
        o_sb = _sb_call(qs, ks, vs)

        x = _out_ffn_call(x2, o_dil.reshape(b * s, dg), o_sb.reshape(b * s, dg), dil_out_norm_w[l],
                          sb_out_norm_w[l], w_out[l], ffn_norm_w[l], w_gate[l], w_up[l],
                          w_down[l]).reshape(b, s, d)
    return x
```

```python
import functools

import jax
import jax.numpy as jnp
from jax import lax
from jax.experimental import pallas as pl
from jax.experimental.pallas import tpu as pltpu

F32 = jnp.float32
BF16 = jnp.bfloat16

HEAD_DIM = 64
HALF = HEAD_DIM // 2
LANES = 128
MXU_DIM = 256
PROJ_SUB = 512
N_BACK = 128
BLOCK = 128
DIL_GROUP = 1
SB_Q = 128
SB_K = MXU_DIM
SB_PAD = SB_K - SB_Q
SB_DEAD = 127.0
LOG2E = 1.4426950408889634
ROPE_THETA = 10000.0
EPS = 1e-6
VMEM_LIMIT = 56 * 1024 * 1024


def _dot(a, b):
    return jnp.dot(a, b, preferred_element_type=F32)


def _dot_nt(a, b):
    return lax.dot_general(a, b, (((1,), (1,)), ((), ())), preferred_element_type=F32)


def _rmsnorm(x, w):
    return x * lax.rsqrt(jnp.mean(x * x, axis=-1, keepdims=True) + EPS) * w


def _software_pipeline(n, stages, between=None):
    order = [0] + list(range(len(stages) - 1, 0, -1))
    for t in range(n + len(stages) - 1):
        for k in order:
            if 0 <= t - k < n:
                stages[k](t - k)
        if between is not None:
            between(t)


def _proj_kernel(x_ref, nw_ref, w_ref, g_ref, qw_ref, kw_ref, cos_ref, sin_ref,
                 qa_ref, ka_ref, va_ref, qs_ref, ks_ref, vs_ref):
    dg = qa_ref.shape[-1]
    lane = lax.broadcasted_iota(jnp.int32, (1, LANES), 1)
    first_half = (lane % HEAD_DIM) < HALF
    scale = HEAD_DIM ** -0.5

    def head_norm_rope(y, w_full, rows):
        cos = cos_ref[rows, :]
        sin = sin_ref[rows, :]
        outs = []
        for j in range(dg // MXU_DIM):
            yc = y[:, j * MXU_DIM:(j + 1) * MXU_DIM]
            ss = _dot((yc * yc).astype(BF16), g_ref[...])
            yn = yc * lax.rsqrt(ss * (1.0 / HEAD_DIM) + EPS) * w_full[:, j * MXU_DIM:(j + 1) * MXU_DIM]
            for i in range(MXU_DIM // LANES):
                v = yn[:, i * LANES:(i + 1) * LANES]
                partner = jnp.where(first_half, pltpu.roll(v, LANES - HALF, 1), pltpu.roll(v, HALF, 1))
                outs.append(v * cos + partner * sin)
        return jnp.concatenate(outs, axis=1)

    posts = ((qa_ref, lambda y, rows: head_norm_rope(y, qw_ref[...], rows) * scale),
             (ka_ref, lambda y, rows: head_norm_rope(y, kw_ref[...], rows)),
             (va_ref, lambda y, rows: y),
             (qs_ref, lambda y, rows: y * scale),
             (ks_ref, lambda y, rows: y),
             (vs_ref, lambda y, rows: y))

    n_parts = x_ref.shape[0] // PROJ_SUB
    hs = {}

    def normed(part):
        if part not in hs:
            hs[part] = _rmsnorm(x_ref[pl.ds(part * PROJ_SUB, PROJ_SUB), :], nw_ref[...]).astype(BF16)
        return hs[part]

    units = [(part, n) for part in range(n_parts) for n in range(len(posts))]
    matmul = lambda part, n: _dot(normed(part), w_ref[:, n * dg:(n + 1) * dg])
    ys = [matmul(*units[0])]
    for u, (part, n) in enumerate(units):
        if u + 1 < len(units):
            ys.append(matmul(*units[u + 1]))
        if n == 0 and part + 1 < n_parts:
            normed(part + 1)
        rows = pl.ds(part * PROJ_SUB, PROJ_SUB)
        ref, post = posts[n]
        ref[rows, :] = post(ys[u], rows).astype(BF16)


def _proj_call(x2, attn_norm_w, w_in, q_norm_w, k_norm_w, seq):
    t, d = x2.shape
    dg = w_in.shape[1] // 6
    tm = 2 * PROJ_SUB
    n_heads = dg // HEAD_DIM
    pos = jnp.arange(seq, dtype=F32)
    inv_freq = ROPE_THETA ** (-jnp.arange(0, HEAD_DIM, 2, dtype=F32) / HEAD_DIM)
    ang = pos[:, None] * inv_freq[None, :]
    cos_t = jnp.tile(jnp.cos(ang), (1, LANES // HALF))
    sin_t = jnp.tile(jnp.concatenate([-jnp.sin(ang), jnp.sin(ang)], axis=1), (1, LANES // HEAD_DIM))
    hid = jnp.arange(MXU_DIM) // HEAD_DIM
    g = (hid[:, None] == hid[None, :]).astype(BF16)
    qw = jnp.tile(q_norm_w.reshape(1, HEAD_DIM), (1, n_heads))
    kw = jnp.tile(k_norm_w.reshape(1, HEAD_DIM), (1, n_heads))
    tiles_per_seq = seq // tm
    row = lambda i: (i, 0)
    const = lambda i: (0, 0)
    out = jax.ShapeDtypeStruct((t, dg), BF16)
    return pl.pallas_call(
        _proj_kernel,
        grid=(t // tm,),
        in_specs=[
            pl.BlockSpec((tm, d), row),
            pl.BlockSpec((1, d), const),
            pl.BlockSpec(w_in.shape, const, pipeline_mode=pl.Buffered(1)),
            pl.BlockSpec(g.shape, const),
            pl.BlockSpec((1, dg), const),
            pl.BlockSpec((1, dg), const),
            pl.BlockSpec((tm, LANES), lambda i: (i % tiles_per_seq, 0)),
            pl.BlockSpec((tm, LANES), lambda i: (i % tiles_per_seq, 0)),
        ],
        out_specs=[pl.BlockSpec((tm, dg), row)] * 6,
        out_shape=[out] * 6,
        compiler_params=pltpu.CompilerParams(
            dimension_semantics=("arbitrary",), vmem_limit_bytes=VMEM_LIMIT),
        name="proj",
    )(x2, attn_norm_w.reshape(1, d), w_in.astype(BF16), g, qw, kw, cos_t, sin_t)


def _dilated_kernel(q_ref, k_ref, v_ref, o_ref, f_ref, f4_ref, q4_ref, k4_ref, v4_ref,
                    q16_ref, k16_ref, v16_ref, bias_ref, p_ref, md_ref, *state_refs):
    sa_refs, sb_refs = state_refs[:3], state_refs[3:]
    s = q_ref.shape[1]
    quarter, sixteenth = s // 4, s // 16
    n_blocks = s // BLOCK

    prep = []
    for n, (src, d4, d16) in enumerate(((q_ref, q4_ref, q16_ref), (k_ref, k4_ref, k16_ref),
                                        (v_ref, v4_ref, v16_ref))):
        def widen(c, n=n, src=src):
            rows = pl.ds(c * quarter, quarter)
            f_ref[n, rows, :] = src[0, rows, :].astype(F32)

        def by4(c, n=n, d4=d4):
            chunk = f_ref[n, pl.ds(c, quarter, stride=4), :]
            f4_ref[n, pl.ds(c * quarter, quarter), :] = chunk
            d4[pl.ds(c * quarter, quarter), :] = chunk.astype(BF16)

        def by16(c, n=n, d16=d16):
            chunk = f4_ref[n, pl.ds((c % 4) * quarter + c // 4, sixteenth, stride=4), :]
            d16[pl.ds(c * sixteenth, sixteenth), :] = chunk.astype(BF16)

        prep += [functools.partial(widen, c) for c in range(4)]
        prep += [functools.partial(by4, c) for c in range(4)]
        prep += [functools.partial(by16, c) for c in range(16)]

    lane = lax.broadcasted_iota(jnp.int32, (1, LANES), 1)
    head0 = lane < HEAD_DIM
    qi = lax.broadcasted_iota(jnp.int32, (2 * BLOCK, 2 * BLOCK), 0) % BLOCK
    kj = lax.broadcasted_iota(jnp.int32, (2 * BLOCK, 2 * BLOCK), 1)
    dist = qi + BLOCK - kj
    bias_ref[...] = jnp.where((dist >= 0) & (dist <= N_BACK), 0.0, -jnp.inf)

    def unstack(a):
        return jnp.where(head0, a[:BLOCK], a[BLOCK:])

    def fold(state, new):
        (acc_p, m_p, den_p), (acc, m, den) = state, new
        mx = jnp.maximum(m_p, m)
        w_p = jnp.exp2(m_p - mx)
        w_n = jnp.exp2(m - mx)
        return acc_p * w_p + acc * w_n, mx, den_p * w_p + den * w_n

    def keep1(start, new):
        for ref, val in zip(sa_refs, new):
            ref[pl.ds(start, BLOCK), :] = val

    def fold4(start, new):
        if start == 0:
            for c in range(4):
                for a_ref, b_ref in zip(sa_refs, sb_refs):
                    b_ref[pl.ds(c * quarter, quarter), :] = a_ref[pl.ds(c, quarter, stride=4), :]
        rows = pl.ds(start, BLOCK)
        for ref, val in zip(sb_refs, fold([ref[rows, :] for ref in sb_refs], new)):
            ref[rows, :] = val

    def finish16(start, new):
        if start == 0:
            for c in range(16):
                src = pl.ds((c % 4) * quarter + c // 4, sixteenth, stride=4)
                for a_ref, b_ref in zip(sa_refs, sb_refs):
                    a_ref[pl.ds(c * sixteenth, sixteenth), :] = b_ref[src, :]
        acc, _, den = fold([ref[pl.ds(start, BLOCK), :] for ref in sa_refs], new)
        sb_refs[0][pl.ds(start, BLOCK), :] = acc / den

    nat = lambda ref: ref.at[0]
    branches = ((nat(q_ref), nat(k_ref), nat(v_ref), n_blocks, keep1),
                (q4_ref, k4_ref, v4_ref, quarter // BLOCK, fold4),
                (q16_ref, k16_ref, v16_ref, sixteenth // BLOCK, finish16))
    groups_per_branch = n_blocks // DIL_GROUP
    n_groups = len(branches) * groups_per_branch

    def blocks_of(g):
        branch = branches[g // groups_per_branch]
        out = []
        for j in range(DIL_GROUP):
            blk = (g % groups_per_branch) * DIL_GROUP + j
            start = blk * BLOCK
            first = blk % branch[3] == 0
            out.append((branch, start, (start, BLOCK) if first else (start - BLOCK, 2 * BLOCK), first))
        return out

    def stage_softmax(g, slot):
        for j, ((q_src, k_src, _, _, _), start, (k0, kn), first) in enumerate(blocks_of(g)):
            q = q_src[pl.ds(start, BLOCK), :]
            zero = jnp.zeros_like(q)
            qs = jnp.concatenate([jnp.where(head0, q, zero), jnp.where(head0, zero, q)], axis=0)
            bias = bias_ref[:, BLOCK:] if first else bias_ref[...]
            sc = _dot_nt(qs, k_src[pl.ds(k0, kn), :]) * LOG2E + bias
            m = jnp.max(sc, axis=-1, keepdims=True)
            p = jnp.exp2(sc - m)
            p_ref[slot, j, :, pl.ds(0, kn)] = p.astype(BF16)
            md_ref[slot, j, 0] = unstack(m)
            md_ref[slot, j, 1] = unstack(jnp.sum(p, axis=-1, keepdims=True))

    def stage_values(g, slot):
        for j, ((_, _, v_src, _, post), start, (k0, kn), _) in enumerate(blocks_of(g)):
            acc = unstack(_dot(p_ref[slot, j, :, pl.ds(0, kn)], v_src[pl.ds(k0, kn), :]))
            post(start, (acc, md_ref[slot, j, 0], md_ref[slot, j, 1]))

    per_step = -(-len(prep) // (groups_per_branch - 2))

    def between(step):
        for piece in prep[step * per_step:(step + 1) * per_step]:
            piece()

    _software_pipeline(n_groups, [lambda g, stage=stage: stage(g, g % 2)
                                  for stage in (stage_softmax, stage_values)], between)

    o16_ref, o4_ref = sb_refs[0], sa_refs[0]
    for c in range(16):
        o4_ref[pl.ds((c % 4) * quarter + c // 4, sixteenth, stride=4), :] = (
            o16_ref[pl.ds(c * sixteenth, sixteenth), :])
    for c in range(4):
        o_ref[0, pl.ds(c, quarter, stride=4), :] = o4_ref[pl.ds(c * quarter, quarter), :]


def _dilated_call(qa, ka, va):
    b, s, dg = qa.shape
    pairs = dg // LANES
    spec = pl.BlockSpec((1, s, LANES), lambda bi, hp: (bi, 0, hp))
    slab = lambda dt: pltpu.VMEM((s, LANES), dt)
    return pl.pallas_call(
        _dilated_kernel,
        grid=(b, pairs),
        in_specs=[spec] * 3,
        out_specs=spec,
        out_shape=jax.ShapeDtypeStruct((b, s, dg), F32),
        scratch_shapes=([pltpu.VMEM((3, s, LANES), F32)] * 2 + [slab(BF16)] * 6 + [
                        pltpu.VMEM((2 * BLOCK, 2 * BLOCK), F32),
                        pltpu.VMEM((2, DIL_GROUP, 2 * BLOCK, 2 * BLOCK), BF16),
                        pltpu.VMEM((2, DIL_GROUP, 2, BLOCK, LANES), F32)] + [slab(F32)] * 6),
        compiler_params=pltpu.CompilerParams(
            dimension_semantics=("arbitrary",) * 2, vmem_limit_bytes=VMEM_LIMIT),
        name="dilated",
    )(qa, ka, va)


def _sb_kernel(q_ref, k_ref, v_ref, u_ref, o_ref, kp_ref, vp_ref, z_ref, w_ref, shift_ref, live_ref):
    s = q_ref.shape[1]
    kp_ref[pl.ds(0, SB_PAD), :] = jnp.zeros((SB_PAD, LANES), BF16)
    vp_ref[pl.ds(0, SB_PAD), :] = jnp.zeros((SB_PAD, LANES), BF16)
    kp_ref[pl.ds(SB_PAD, s), :] = k_ref[0]
    vp_ref[pl.ds(SB_PAD, s), :] = v_ref[0]

    lane = lax.broadcasted_iota(jnp.int32, (1, LANES), 1)
    head0 = lane < HEAD_DIM
    ri = lax.broadcasted_iota(jnp.int32, (2 * SB_Q, SB_K), 0) % SB_Q
    cj = lax.broadcasted_iota(jnp.int32, (2 * SB_Q, SB_K), 1)
    causal = cj < ri + (SB_K - SB_Q)
    u = u_ref[...]

    def neg_log_keep(z, mask):
        nk = jnp.maximum(z, 0.0) + jnp.log2(1.0 + jnp.exp2(-jnp.abs(z)))
        return nk if mask is None else jnp.where(mask, nk, 0.0)

    def inclusive_suffix(nk):
        hi = nk.astype(BF16)
        lo = (nk - hi.astype(F32)).astype(BF16)
        cs = _dot(jnp.concatenate([hi, lo], axis=0), u)
        return cs[:2 * SB_Q] + cs[2 * SB_Q:]

    def weights(z, cs, shift, mask):
        x = z - cs
        if shift is not None:
            x = x - shift
        a = jnp.exp2(x)
        if mask is not None:
            a = jnp.where(mask, a, 0.0)
        return a.astype(BF16)

    def live(shift):
        return jnp.min(shift) < SB_DEAD

    def block_start(qb):
        return qb * SB_Q if isinstance(qb, int) else pl.multiple_of(qb * SB_Q, SB_Q)

    def stacked_q(qb):
        q = q_ref[0, pl.ds(block_start(qb), SB_Q), :]
        zero = jnp.zeros_like(q)
        return jnp.concatenate([jnp.where(head0, q, zero), jnp.where(head0, zero, q)], axis=0)

    def tile_keys(qb, i):
        return pl.ds(block_start(qb - 1 + SB_PAD // SB_Q - 2 * i), SB_K)

    def score(qb, i):
        return _dot_nt(stacked_q(qb), kp_ref[tile_keys(qb, i), :])

    def unstack(acc):
        return jnp.where(head0, acc[:SB_Q], acc[SB_Q:])

    def stage_scores(qb):
        z_ref[qb % 2] = score(qb, 0)

    def stage_weights(qb):
        z = z_ref[qb % 2] * LOG2E
        nk = neg_log_keep(z, causal)
        w_ref[qb % 2] = weights(z, inclusive_suffix(nk), None, causal)
        shift = jnp.sum(nk, axis=-1, keepdims=True)
        shift_ref[qb] = shift
        live_ref[qb] = live(shift).astype(jnp.int32)

    def stage_values(qb):
        o_ref[0, pl.ds(qb * SB_Q, SB_Q), :] = unstack(_dot(w_ref[qb % 2], vp_ref[tile_keys(qb, 0), :]))

    _software_pipeline(s // SB_Q, [stage_scores, stage_weights, stage_values])

    def tail(qb):
        def cond(state):
            return (state[0] <= qb // 2) & state[1]

        def body(state):
            i, _, acc, shift = state
            z = score(qb, i) * LOG2E
            nk = neg_log_keep(z, None)
            w = weights(z, inclusive_suffix(nk), shift, None)
            shift = shift + jnp.sum(nk, axis=-1, keepdims=True)
            return i + 1, live(shift), acc + _dot(w, vp_ref[tile_keys(qb, i), :]), shift

        state = (jnp.int32(1), jnp.bool_(True), jnp.zeros((2 * SB_Q, LANES), F32), shift_ref[qb])
        o_ref[0, pl.ds(block_start(qb), SB_Q), :] += unstack(lax.while_loop(cond, body, state)[2])

    def tail_body(qb, carry):
        @pl.when(live_ref[qb] != 0)
        def _():
            tail(qb)
        return carry
    lax.fori_loop(0, s // SB_Q, tail_body, 0)


def _sb_call(qs, ks, vs):
    b, s, dg = qs.shape
    pairs = dg // LANES
    idx = jnp.arange(SB_K)
    u = (idx[:, None] >= idx[None, :]).astype(BF16)
    spec = pl.BlockSpec((1, s, LANES), lambda bi, hp: (bi, 0, hp))
    return pl.pallas_call(
        _sb_kernel,
        grid=(b, pairs),
        in_specs=[spec, spec, spec, pl.BlockSpec((SB_K, SB_K), lambda bi, hp: (0, 0))],
        out_specs=spec,
        out_shape=jax.ShapeDtypeStruct((b, s, dg), F32),
        scratch_shapes=[pltpu.VMEM((s + SB_PAD, LANES), BF16)] * 2 + [
            pltpu.VMEM((2, 2 * SB_Q, SB_K), F32),
            pltpu.VMEM((2, 2 * SB_Q, SB_K), BF16),
            pltpu.VMEM((s // SB_Q, 2 * SB_Q, 1), F32),
            pltpu.SMEM((s // SB_Q,), jnp.int32)],
        compiler_params=pltpu.CompilerParams(
            dimension_semantics=("arbitrary",) * 2, vmem_limit_bytes=VMEM_LIMIT),
        name="stick_breaking",
    )(qs, ks, vs, u)


def _out_ffn_kernel(x_ref, od_ref, os_ref, dw_ref, sw_ref, wo_ref, fw_ref, wg_ref, wu_ref, wd_ref,
                    out_ref, x1_ref, act_ref, *, f_chunk):
    dg = od_ref.shape[-1]
    nd = _rmsnorm(od_ref[...], dw_ref[...]).astype(BF16)
    ns = _rmsnorm(os_ref[...], sw_ref[...]).astype(BF16)
    x1 = x_ref[...] + _dot(nd, wo_ref[:dg, :]) + _dot(ns, wo_ref[dg:, :])
    h = _rmsnorm(x1, fw_ref[...]).astype(BF16)
    x1_ref[...] = x1

    def gate_up(i):
        cols = slice(i * f_chunk, (i + 1) * f_chunk)
        return _dot(h, wg_ref[:, cols]), _dot(h, wu_ref[:, cols])

    n_chunks = wg_ref.shape[1] // f_chunk
    nxt = gate_up(0)
    for i in range(n_chunks):
        g, up = nxt
        if i + 1 < n_chunks:
            nxt = gate_up(i + 1)
        act_ref[:, i * f_chunk:(i + 1) * f_chunk] = (g / (1.0 + jnp.exp(-g)) * up).astype(BF16)
    out_ref[...] = x1_ref[...] + _dot(act_ref[...], wd_ref[...])


def _out_ffn_call(x2, o_dil, o_sb, dil_w, sb_w, w_out, ffn_w, w_gate, w_up, w_down):
    t, d = x2.shape
    dg = o_dil.shape[1]
    d_ff = w_gate.shape[1]
    tm = 512
    f_chunk = 256
    row = lambda i: (i, 0)
    const = lambda i: (0, 0)
    resident = lambda shape: pl.BlockSpec(shape, const, pipeline_mode=pl.Buffered(1))
    return pl.pallas_call(
        functools.partial(_out_ffn_kernel, f_chunk=f_chunk),
        grid=(t // tm,),
        in_specs=[
            pl.BlockSpec((tm, d), row),
            pl.BlockSpec((tm, dg), row),
            pl.BlockSpec((tm, dg), row),
            pl.BlockSpec((1, dg), const),
            pl.BlockSpec((1, dg), const),
            resident((2 * dg, d)),
            pl.BlockSpec((1, d), const),
            resident((d, d_ff)),
            resident((d, d_ff)),
            resident((d_ff, d)),
        ],
        out_specs=pl.BlockSpec((tm, d), row),
        out_shape=jax.ShapeDtypeStruct((t, d), F32),
        scratch_shapes=[pltpu.VMEM((tm, d), F32), pltpu.VMEM((tm, d_ff), BF16)],
        compiler_params=pltpu.CompilerParams(
            dimension_semantics=("arbitrary",), vmem_limit_bytes=VMEM_LIMIT,
            allow_input_fusion=[False] * 5 + [True, False, True, True, True]),
        name="out_ffn",
    )(x2, o_dil, o_sb, dil_w.reshape(1, dg), sb_w.reshape(1, dg), w_out.astype(BF16),
      ffn_w.reshape(1, d), w_gate.astype(BF16), w_up.astype(BF16), w_down.astype(BF16))


def kernel(x, attn_norm_w, w_in, q_norm_w, k_norm_w, dil_out_norm_w, sb_out_norm_w, w_out,
           ffn_norm_w, w_gate, w_up, w_down):
    b, s, d = x.shape
    for l in range(attn_norm_w.shape[0]):
        x2 = x.reshape(b * s, d)
        qa, ka, va, qs, ks, vs = _proj_call(x2, attn_norm_w[l], w_in[l], q_norm_w[l], k_norm_w[l], s)
        dg = qa.shape[1]
        qa, ka, va, qs, ks, vs = (a.reshape(b, s, dg) for a in (qa, ka, va, qs, ks, vs))

        o_dil = _dilated_call(qa, ka, va)
        o_sb = _sb_call(qs, ks, vs)

        x = _out_ffn_call(x2, o_dil.reshape(b * s, dg), o_sb.reshape(b * s, dg), dil_out_norm_w[l],
                          sb_out_norm_w[l], w_out[l], ffn_norm_w[l], w_gate[l], w_up[l],
                          w_down[l]).reshape(b, s, d)
    return x
```

```python
import functools

import jax
import jax.numpy as jnp
from jax import lax
from jax.experimental import pallas as pl
from jax.experimental.pallas import tpu as pltpu

F32 = jnp.float32
BF16 = jnp.bfloat16

HEAD_DIM = 64
HALF = HEAD_DIM // 2
LANES = 128
MXU_DIM = 256
PROJ_SUB = 512
N_BACK = 128
BLOCK = 128
DIL_GROUP = 1
SB_Q = 128
SB_K = MXU_DIM
SB_PAD = SB_K - SB_Q
SB_DEAD = 127.0
LOG2E = 1.4426950408889634
ROPE_THETA = 10000.0
EPS = 1e-6
VMEM_LIMIT = 56 * 1024 * 1024


def _dot(a, b):
    return jnp.dot(a, b, preferred_element_type=F32)


def _dot_nt(a, b):
    return lax.dot_general(a, b, (((1,), (1,)), ((), ())), preferred_element_type=F32)


def _rmsnorm(x, w):
    return x * lax.rsqrt(jnp.mean(x * x, axis=-1, keepdims=True) + EPS) * w


def _software_pipeline(n, stages, between=None):
    order = [0] + list(range(len(stages) - 1, 0, -1))
    for t in range(n + len(stages) - 1):
        for k in order:
            if 0 <= t - k < n:
                stages[k](t - k)
        if between is not None:
            between(t)


def _proj_kernel(x_ref, nw_ref, w_ref, g_ref, qw_ref, kw_ref, cos_ref, sin_ref,
                 qa_ref, ka_ref, va_ref, qs_ref, ks_ref, vs_ref):
    dg = qa_ref.shape[-1]
    lane = lax.broadcasted_iota(jnp.int32, (1, LANES), 1)
    first_half = (lane % HEAD_DIM) < HALF
    scale = HEAD_DIM ** -0.5

    def head_norm_rope(y, w_full, rows):
        cos = cos_ref[rows, :]
        sin = sin_ref[rows, :]
        outs = []
        for j in range(dg // MXU_DIM):
            yc = y[:, j * MXU_DIM:(j + 1) * MXU_DIM]
            ss = _dot((yc * yc).astype(BF16), g_ref[...])
            yn = yc * lax.rsqrt(ss * (1.0 / HEAD_DIM) + EPS) * w_full[:, j * MXU_DIM:(j + 1) * MXU_DIM]
            for i in range(MXU_DIM // LANES):
                v = yn[:, i * LANES:(i + 1) * LANES]
                partner = jnp.where(first_half, pltpu.roll(v, LANES - HALF, 1), pltpu.roll(v, HALF, 1))
                outs.append(v * cos + partner * sin)
        return jnp.concatenate(outs, axis=1)

    posts = ((qa_ref, lambda y, rows: head_norm_rope(y, qw_ref[...], rows) * scale),
             (ka_ref, lambda y, rows: head_norm_rope(y, kw_ref[...], rows)),
             (va_ref, lambda y, rows: y),
             (qs_ref, lambda y, rows: y * scale),
             (ks_ref, lambda y, rows: y),
             (vs_ref, lambda y, rows: y))

    n_parts = x_ref.shape[0] // PROJ_SUB
    hs = {}

    def normed(part):
        if part not in hs:
            hs[part] = _rmsnorm(x_ref[pl.ds(part * PROJ_SUB, PROJ_SUB), :], nw_ref[...]).astype(BF16)
        return hs[part]

    units = [(part, n) for part in range(n_parts) for n in range(len(posts))]
    matmul = lambda part, n: _dot(normed(part), w_ref[:, n * dg:(n + 1) * dg])
    ys = [matmul(*units[0])]
    for u, (part, n) in enumerate(units):
        if u + 1 < len(units):
            ys.append(matmul(*units[u + 1]))
        if n == 0 and part + 1 < n_parts:
            normed(part + 1)
        rows = pl.ds(part * PROJ_SUB, PROJ_SUB)
        ref, post = posts[n]
        ref[rows, :] = post(ys[u], rows).astype(BF16)


def _proj_call(x2, attn_norm_w, w_in, q_norm_w, k_norm_w, seq):
    t, d = x2.shape
    dg = w_in.shape[1] // 6
    tm = 2 * PROJ_SUB
    n_heads = dg // HEAD_DIM
    pos = jnp.arange(seq, dtype=F32)
    inv_freq = ROPE_THETA ** (-jnp.arange(0, HEAD_DIM, 2, dtype=F32) / HEAD_DIM)
    ang = pos[:, None] * inv_freq[None, :]
    cos_t = jnp.tile(jnp.cos(ang), (1, LANES // HALF))
    sin_t = jnp.tile(jnp.concatenate([-jnp.sin(ang), jnp.sin(ang)], axis=1), (1, LANES // HEAD_DIM))
    hid = jnp.arange(MXU_DIM) // HEAD_DIM
    g = (hid[:, None] == hid[None, :]).astype(BF16)
    qw = jnp.tile(q_norm_w.reshape(1, HEAD_DIM), (1, n_heads))
    kw = jnp.tile(k_norm_w.reshape(1, HEAD_DIM), (1, n_heads))
    tiles_per_seq = seq // tm
    row = lambda i: (i, 0)
    const = lambda i: (0, 0)
    out = jax.ShapeDtypeStruct((t, dg), BF16)
    return pl.pallas_call(
        _proj_kernel,
        grid=(t // tm,),
        in_specs=[
            pl.BlockSpec((tm, d), row),
            pl.BlockSpec((1, d), const),
            pl.BlockSpec(w_in.shape, const, pipeline_mode=pl.Buffered(1)),
            pl.BlockSpec(g.shape, const),
            pl.BlockSpec((1, dg), const),
            pl.BlockSpec((1, dg), const),
            pl.BlockSpec((tm, LANES), lambda i: (i % tiles_per_seq, 0)),
            pl.BlockSpec((tm, LANES), lambda i: (i % tiles_per_seq, 0)),
        ],
        out_specs=[pl.BlockSpec((tm, dg), row)] * 6,
        out_shape=[out] * 6,
        compiler_params=pltpu.CompilerParams(
            dimension_semantics=("parallel",), vmem_limit_bytes=VMEM_LIMIT),
        name="proj",
    )(x2, attn_norm_w.reshape(1, d), w_in.astype(BF16), g, qw, kw, cos_t, sin_t)


def _dilated_kernel(q_ref, k_ref, v_ref, o_ref, f_ref, f4_ref, q4_ref, k4_ref, v4_ref,
                    q16_ref, k16_ref, v16_ref, bias_ref, p_ref, md_ref, *state_refs):
    sa_refs, sb_refs = state_refs[:3], state_refs[3:]
    s = q_ref.shape[1]
    quarter, sixteenth = s // 4, s // 16
    n_blocks = s // BLOCK

    prep = []
    for n, (src, d4, d16) in enumerate(((q_ref, q4_ref, q16_ref), (k_ref, k4_ref, k16_ref),
                                        (v_ref, v4_ref, v16_ref))):
        def widen(c, n=n, src=src):
            rows = pl.ds(c * quarter, quarter)
            f_ref[n, rows, :] = src[0, rows, :].astype(F32)

        def by4(c, n=n, d4=d4):
            chunk = f_ref[n, pl.ds(c, quarter, stride=4), :]
            f4_ref[n, pl.ds(c * quarter, quarter), :] = chunk
            d4[pl.ds(c * quarter, quarter), :] = chunk.astype(BF16)

        def by16(c, n=n, d16=d16):
            chunk = f4_ref[n, pl.ds((c % 4) * quarter + c // 4, sixteenth, stride=4), :]
            d16[pl.ds(c * sixteenth, sixteenth), :] = chunk.astype(BF16)

        prep += [functools.partial(widen, c) for c in range(4)]
        prep += [functools.partial(by4, c) for c in range(4)]
        prep += [functools.partial(by16, c) for c in range(16)]

    lane = lax.broadcasted_iota(jnp.int32, (1, LANES), 1)
    head0 = lane < HEAD_DIM
    qi = lax.broadcasted_iota(jnp.int32, (2 * BLOCK, 2 * BLOCK), 0) % BLOCK
    kj = lax.broadcasted_iota(jnp.int32, (2 * BLOCK, 2 * BLOCK), 1)
    dist = qi + BLOCK - kj
    bias_ref[...] = jnp.where((dist >= 0) & (dist <= N_BACK), 0.0, -jnp.inf)

    def unstack(a):
        return jnp.where(head0, a[:BLOCK], a[BLOCK:])

    def fold(state, new):
        (acc_p, m_p, den_p), (acc, m, den) = state, new
        mx = jnp.maximum(m_p, m)
        w_p = jnp.exp2(m_p - mx)
        w_n = jnp.exp2(m - mx)
        return acc_p * w_p + acc * w_n, mx, den_p * w_p + den * w_n

    def keep1(start, new):
        for ref, val in zip(sa_refs, new):
            ref[pl.ds(start, BLOCK), :] = val

    def fold4(start, new):
        if start == 0:
            for c in range(4):
                for a_ref, b_ref in zip(sa_refs, sb_refs):
                    b_ref[pl.ds(c * quarter, quarter), :] = a_ref[pl.ds(c, quarter, stride=4), :]
        rows = pl.ds(start, BLOCK)
        for ref, val in zip(sb_refs, fold([ref[rows, :] for ref in sb_refs], new)):
            ref[rows, :] = val

    def finish16(start, new):
        if start == 0:
            for c in range(16):
                src = pl.ds((c % 4) * quarter + c // 4, sixteenth, stride=4)
                for a_ref, b_ref in zip(sa_refs, sb_refs):
                    a_ref[pl.ds(c * sixteenth, sixteenth), :] = b_ref[src, :]
        acc, _, den = fold([ref[pl.ds(start, BLOCK), :] for ref in sa_refs], new)
        sb_refs[0][pl.ds(start, BLOCK), :] = acc / den

    nat = lambda ref: ref.at[0]
    branches = ((nat(q_ref), nat(k_ref), nat(v_ref), n_blocks, keep1),
                (q4_ref, k4_ref, v4_ref, quarter // BLOCK, fold4),
                (q16_ref, k16_ref, v16_ref, sixteenth // BLOCK, finish16))
    groups_per_branch = n_blocks // DIL_GROUP
    n_groups = len(branches) * groups_per_branch

    def blocks_of(g):
        branch = branches[g // groups_per_branch]
        out = []
        for j in range(DIL_GROUP):
            blk = (g % groups_per_branch) * DIL_GROUP + j
            start = blk * BLOCK
            first = blk % branch[3] == 0
            out.append((branch, start, (start, BLOCK) if first else (start - BLOCK, 2 * BLOCK), first))
        return out

    def stage_softmax(g, slot):
        for j, ((q_src, k_src, _, _, _), start, (k0, kn), first) in enumerate(blocks_of(g)):
            q = q_src[pl.ds(start, BLOCK), :]
            zero = jnp.zeros_like(q)
            qs = jnp.concatenate([jnp.where(head0, q, zero), jnp.where(head0, zero, q)], axis=0)
            bias = bias_ref[:, BLOCK:] if first else bias_ref[...]
            sc = _dot_nt(qs, k_src[pl.ds(k0, kn), :]) * LOG2E + bias
            m = jnp.max(sc, axis=-1, keepdims=True)
            p = jnp.exp2(sc - m)
            p_ref[slot, j, :, pl.ds(0, kn)] = p.astype(BF16)
            md_ref[slot, j, 0] = unstack(m)
            md_ref[slot, j, 1] = unstack(jnp.sum(p, axis=-1, keepdims=True))

    def stage_values(g, slot):
        for j, ((_, _, v_src, _, post), start, (k0, kn), _) in enumerate(blocks_of(g)):
            acc = unstack(_dot(p_ref[slot, j, :, pl.ds(0, kn)], v_src[pl.ds(k0, kn), :]))
            post(start, (acc, md_ref[slot, j, 0], md_ref[slot, j, 1]))

    per_step = -(-len(prep) // (groups_per_branch - 2))

    def between(step):
        for piece in prep[step * per_step:(step + 1) * per_step]:
            piece()

    _software_pipeline(n_groups, [lambda g, stage=stage: stage(g, g % 2)
                                  for stage in (stage_softmax, stage_values)], between)

    o16_ref, o4_ref = sb_refs[0], sa_refs[0]
    for c in range(16):
        o4_ref[pl.ds((c % 4) * quarter + c // 4, sixteenth, stride=4), :] = (
            o16_ref[pl.ds(c * sixteenth, sixteenth), :])
    for c in range(4):
        o_ref[0, pl.ds(c, quarter, stride=4), :] = o4_ref[pl.ds(c * quarter, quarter), :]


def _dilated_call(qa, ka, va):
    b, s, dg = qa.shape
    pairs = dg // LANES
    spec = pl.BlockSpec((1, s, LANES), lambda bi, hp: (bi, 0, hp))
    slab = lambda dt: pltpu.VMEM((s, LANES), dt)
    return pl.pallas_call(
        _dilated_kernel,
        grid=(b, pairs),
        in_specs=[spec] * 3,
        out_specs=spec,
        out_shape=jax.ShapeDtypeStruct((b, s, dg), F32),
        scratch_shapes=([pltpu.VMEM((3, s, LANES), F32)] * 2 + [slab(BF16)] * 6 + [
                        pltpu.VMEM((2 * BLOCK, 2 * BLOCK), F32),
                        pltpu.VMEM((2, DIL_GROUP, 2 * BLOCK, 2 * BLOCK), BF16),
                        pltpu.VMEM((2, DIL_GROUP, 2, BLOCK, LANES), F32)] + [slab(F32)] * 6),
        compiler_params=pltpu.CompilerParams(
            dimension_semantics=("arbitrary",) * 2, vmem_limit_bytes=VMEM_LIMIT),
        name="dilated",
    )(qa, ka, va)


def _sb_kernel(q_ref, k_ref, v_ref, u_ref, o_ref, kp_ref, vp_ref, z_ref, w_ref, shift_ref, live_ref):
    s = q_ref.shape[1]
    kp_ref[pl.ds(0, SB_PAD), :] = jnp.zeros((SB_PAD, LANES), BF16)
    vp_ref[pl.ds(0, SB_PAD), :] = jnp.zeros((SB_PAD, LANES), BF16)
    kp_ref[pl.ds(SB_PAD, s), :] = k_ref[0]
    vp_ref[pl.ds(SB_PAD, s), :] = v_ref[0]

    lane = lax.broadcasted_iota(jnp.int32, (1, LANES), 1)
    head0 = lane < HEAD_DIM
    ri = lax.broadcasted_iota(jnp.int32, (2 * SB_Q, SB_K), 0) % SB_Q
    cj = lax.broadcasted_iota(jnp.int32, (2 * SB_Q, SB_K), 1)
    causal = cj < ri + (SB_K - SB_Q)
    u = u_ref[...]

    def neg_log_keep(z, mask):
        nk = jnp.maximum(z, 0.0) + jnp.log2(1.0 + jnp.exp2(-jnp.abs(z)))
        return nk if mask is None else jnp.where(mask, nk, 0.0)

    def inclusive_suffix(nk):
        hi = nk.astype(BF16)
        lo = (nk - hi.astype(F32)).astype(BF16)
        cs = _dot(jnp.concatenate([hi, lo], axis=0), u)
        return cs[:2 * SB_Q] + cs[2 * SB_Q:]

    def weights(z, cs, shift, mask):
        x = z - cs
        if shift is not None:
            x = x - shift
        a = jnp.exp2(x)
        if mask is not None:
            a = jnp.where(mask, a, 0.0)
        return a.astype(BF16)

    def live(shift):
        return jnp.min(shift) < SB_DEAD

    def block_start(qb):
        return qb * SB_Q if isinstance(qb, int) else pl.multiple_of(qb * SB_Q, SB_Q)

    def stacked_q(qb):
        q = q_ref[0, pl.ds(block_start(qb), SB_Q), :]
        zero = jnp.zeros_like(q)
        return jnp.concatenate([jnp.where(head0, q, zero), jnp.where(head0, zero, q)], axis=0)

    def tile_keys(qb, i):
        return pl.ds(block_start(qb - 1 + SB_PAD // SB_Q - 2 * i), SB_K)

    def score(qb, i):
        return _dot_nt(stacked_q(qb), kp_ref[tile_keys(qb, i), :])

    def unstack(acc):
        return jnp.where(head0, acc[:SB_Q], acc[SB_Q:])

    def stage_scores(qb):
        z_ref[qb % 2] = score(qb, 0)

    def stage_weights(qb):
        z = z_ref[qb % 2] * LOG2E
        nk = neg_log_keep(z, causal)
        w_ref[qb % 2] = weights(z, inclusive_suffix(nk), None, causal)
        shift = jnp.sum(nk, axis=-1, keepdims=True)
        shift_ref[qb] = shift
        live_ref[qb] = live(shift).astype(jnp.int32)

    def stage_values(qb):
        o_ref[0, pl.ds(qb * SB_Q, SB_Q), :] = unstack(_dot(w_ref[qb % 2], vp_ref[tile_keys(qb, 0), :]))

    _software_pipeline(s // SB_Q, [stage_scores, stage_weights, stage_values])

    def tail(qb):
        def cond(state):
            return (state[0] <= qb // 2) & state[1]

        def body(state):
            i, _, acc, shift = state
            z = score(qb, i) * LOG2E
            nk = neg_log_keep(z, None)
            w = weights(z, inclusive_suffix(nk), shift, None)
            shift = shift + jnp.sum(nk, axis=-1, keepdims=True)
            return i + 1, live(shift), acc + _dot(w, vp_ref[tile_keys(qb, i), :]), shift

        state = (jnp.int32(1), jnp.bool_(True), jnp.zeros((2 * SB_Q, LANES), F32), shift_ref[qb])
        o_ref[0, pl.ds(block_start(qb), SB_Q), :] += unstack(lax.while_loop(cond, body, state)[2])

    def tail_body(qb, carry):
        @pl.when(live_ref[qb] != 0)
        def _():
            tail(qb)
        return carry
    lax.fori_loop(0, s // SB_Q, tail_body, 0)


def _sb_call(qs, ks, vs):
    b, s, dg = qs.shape
    pairs = dg // LANES
    idx = jnp.arange(SB_K)
    u = (idx[:, None] >= idx[None, :]).astype(BF16)
    spec = pl.BlockSpec((1, s, LANES), lambda bi, hp: (bi, 0, hp))
    return pl.pallas_call(
        _sb_kernel,
        grid=(b, pairs),
        in_specs=[spec, spec, spec, pl.BlockSpec((SB_K, SB_K), lambda bi, hp: (0, 0))],
        out_specs=spec,
        out_shape=jax.ShapeDtypeStruct((b, s, dg), F32),
        scratch_shapes=[pltpu.VMEM((s + SB_PAD, LANES), BF16)] * 2 + [
            pltpu.VMEM((2, 2 * SB_Q, SB_K), F32),
            pltpu.VMEM((2, 2 * SB_Q, SB_K), BF16),
            pltpu.VMEM((s // SB_Q, 2 * SB_Q, 1), F32),
            pltpu.SMEM((s // SB_Q,), jnp.int32)],
        compiler_params=pltpu.CompilerParams(
            dimension_semantics=("arbitrary",) * 2, vmem_limit_bytes=VMEM_LIMIT),
        name="stick_breaking",
    )(qs, ks, vs, u)


def _out_ffn_kernel(x_ref, od_ref, os_ref, dw_ref, sw_ref, wo_ref, fw_ref, wg_ref, wu_ref, wd_ref,
                    out_ref, x1_ref, act_ref, *, f_chunk):
    dg = od_ref.shape[-1]
    nd = _rmsnorm(od_ref[...], dw_ref[...]).astype(BF16)
    ns = _rmsnorm(os_ref[...], sw_ref[...]).astype(BF16)
    x1 = x_ref[...] + _dot(nd, wo_ref[:dg, :]) + _dot(ns, wo_ref[dg:, :])
    h = _rmsnorm(x1, fw_ref[...]).astype(BF16)
    x1_ref[...] = x1

    def gate_up(i):
        cols = slice(i * f_chunk, (i + 1) * f_chunk)
        return _dot(h, wg_ref[:, cols]), _dot(h, wu_ref[:, cols])

    n_chunks = wg_ref.shape[1] // f_chunk
    nxt = gate_up(0)
    for i in range(n_chunks):
        g, up = nxt
        if i + 1 < n_chunks:
            nxt = gate_up(i + 1)
        act_ref[:, i * f_chunk:(i + 1) * f_chunk] = (g / (1.0 + jnp.exp(-g)) * up).astype(BF16)
    out_ref[...] = x1_ref[...] + _dot(act_ref[...], wd_ref[...])


def _out_ffn_call(x2, o_dil, o_sb, dil_w, sb_w, w_out, ffn_w, w_gate, w_up, w_down):
    t, d = x2.shape
    dg = o_dil.shape[1]
    d_ff = w_gate.shape[1]
    tm = 512
    f_chunk = 256
    row = lambda i: (i, 0)
    const = lambda i: (0, 0)
    resident = lambda shape: pl.BlockSpec(shape, const, pipeline_mode=pl.Buffered(1))
    return pl.pallas_call(
        functools.partial(_out_ffn_kernel, f_chunk=f_chunk),
        grid=(t // tm,),
        in_specs=[
            pl.BlockSpec((tm, d), row),
            pl.BlockSpec((tm, dg), row),
            pl.BlockSpec((tm, dg), row),
            pl.BlockSpec((1, dg), const),
            pl.BlockSpec((1, dg), const),
            resident((2 * dg, d)),
            pl.BlockSpec((1, d), const),
            resident((d, d_ff)),
            resident((d, d_ff)),
            resident((d_ff, d)),
        ],
        out_specs=pl.BlockSpec((tm, d), row),
        out_shape=jax.ShapeDtypeStruct((t, d), F32),
        scratch_shapes=[pltpu.VMEM((tm, d), F32), pltpu.VMEM((tm, d_ff), BF16)],
        compiler_params=pltpu.CompilerParams(
            dimension_semantics=("parallel",), vmem_limit_bytes=VMEM_LIMIT),
        name="out_ffn",
    )(x2, o_dil, o_sb, dil_w.reshape(1, dg), sb_w.reshape(1, dg), w_out.astype(BF16),
      ffn_w.reshape(1, d), w_gate.astype(BF16), w_up.astype(BF16), w_down.astype(BF16))


def kernel(x, attn_norm_w, w_in, q_norm_w, k_norm_w, dil_out_norm_w, sb_out_norm_w, w_out,
           ffn_norm_w, w_gate, w_up, w_down):
    b, s, d = x.shape
    for l in range(attn_norm_w.shape[0]):
        x2 = x.reshape(b * s, d)
        qa, ka, va, qs, ks, vs = _proj_call(x2, attn_norm_w[l], w_in[l], q_norm_w[l], k_norm_w[l], s)
        dg = qa.shape[1]
        qa, ka, va, qs, ks, vs = (a.reshape(b, s, dg) for a in (qa, ka, va, qs, ks, vs))

        o_dil = _dilated_call(qa, ka, va)
        o_sb = _sb_call(qs, ks, vs)

        x = _out_ffn_call(x2, o_dil.reshape(b * s, dg), o_sb.reshape(b * s, dg), dil_out_norm_w[l],
                          sb_out_norm_w[l], w_out[l], ffn_norm_w[l], w_gate[l], w_up[l],
                          w_down[l]).reshape(b, s, d)
    return x
```
